```python
import jax, jax.numpy as jnp
from jax import lax
import numpy as np

D_MODEL = 1024
BATCH = 8
SEQ = 2048
DEPTH = 4
DEC_BATCH = 32
DEC_SEQ = 4
PAST_LEN = 8192
PAGE_SIZE = 128

N_MIXERS = 3
N_LAYERS_A = (DEPTH + 2) // 3
N_LAYERS_B = (DEPTH + 1) // 3
N_LAYERS_C = DEPTH // 3
H_A = 16
DH_A = D_MODEL // H_A
Q_BLOCK = 128
FORGET_BIAS_MIN = 4.0
FORGET_BIAS_MAX = 10.0
D_RNN = ((4 * D_MODEL // 3 + 127) // 128) * 128
N_BLK_B = 16
BW_B = D_RNN // N_BLK_B
CONV_W = 4
LRU_C = 8.0
H_C = 4
DK_C = D_MODEL // H_C
DV_C = 2 * DK_C
CHUNK_C = 128
ROPE_BASE = 10000.0
D_FF = ((8 * D_MODEL // 3 + 255) // 256) * 256
ALPHA = (2 * DEPTH) ** 0.25
BETA = (8 * DEPTH) ** -0.25
LN_EPS = 1e-5
GN_EPS = 1e-5

kernel_name = "fox_rglru_retnet_hybrid_step"

F32 = jnp.float32


def layer_norm(x, g, b):
    xf = x.astype(F32)
    mu = jnp.mean(xf, -1, keepdims=True)
    var = jnp.mean(jnp.square(xf - mu), -1, keepdims=True)
    return ((xf - mu) * lax.rsqrt(var + LN_EPS) * g + b).astype(x.dtype)


def swiglu_ffn(x, w_up, w_down):
    gate, up = jnp.split(x @ w_up, 2, axis=-1)
    return (jax.nn.silu(gate) * up) @ w_down


def fox_project(x, w_in, b_f):
    B, L, _ = x.shape
    proj = x @ w_in
    qkv = proj[..., :3 * D_MODEL].reshape(B, L, 3, H_A, DH_A)
    logf = jax.nn.log_sigmoid((proj[..., 3 * D_MODEL:] + b_f).astype(F32))
    return qkv[:, :, 0], qkv[:, :, 1], qkv[:, :, 2], logf


def fox_prompt(q, k, v, logf):
    B, L = q.shape[:2]
    scale = DH_A ** -0.5
    ct = jnp.cumsum(logf, axis=1).transpose(0, 2, 1)
    outs = []
    for q0 in range(0, L, Q_BLOCK):
        q1 = min(q0 + Q_BLOCK, L)
        s = jnp.einsum('bqhd,bkhd->bhqk', q[:, q0:q1], k[:, :q1]).astype(F32) * scale
        s = s + ct[:, :, q0:q1, None] - ct[:, :, None, :q1]
        mask = jnp.arange(q0, q1)[:, None] >= jnp.arange(q1)[None, :]
        p = jax.nn.softmax(jnp.where(mask, s, -jnp.inf), axis=-1).astype(v.dtype)
        outs.append(jnp.einsum('bhqk,bkhd->bqhd', p, v[:, :q1]))
    return jnp.concatenate(outs, axis=1).reshape(B, L, H_A * DH_A)


def fox_sample(q, k, v, logf, k_pool, v_pool, lf_pool, page_table):
    Bd, L = q.shape[:2]
    scale = DH_A ** -0.5
    kp = k_pool[page_table].reshape(Bd, -1, H_A, DH_A)
    vp = v_pool[page_table].reshape(Bd, -1, H_A, DH_A)
    lfp = lf_pool[page_table].reshape(Bd, -1, H_A).astype(F32)
    n_past = kp.shape[1]
    suffix = (lax.cumsum(lfp, axis=1, reverse=True) - lfp).transpose(0, 2, 1)
    ct = jnp.cumsum(logf, axis=1).transpose(0, 2, 1)
    s_past = jnp.einsum('bqhd,bkhd->bhqk', q, kp).astype(F32) * scale + ct[..., :, None] + suffix[..., None, :]
    s_new = jnp.einsum('bqhd,bkhd->bhqk', q, k).astype(F32) * scale + ct[..., :, None] - ct[..., None, :]
    causal = jnp.tril(jnp.ones((L, L), dtype=bool))
    s_new = jnp.where(causal, s_new, -jnp.inf)
    p = jax.nn.softmax(jnp.concatenate([s_past, s_new], axis=-1), axis=-1).astype(v.dtype)
    o = jnp.einsum('bhqk,bkhd->bqhd', p[..., :n_past], vp) + jnp.einsum('bhqk,bkhd->bqhd', p[..., n_past:], v)
    return o.reshape(Bd, L, H_A * DH_A)


def linear_scan(a, u, h0):
    def combine(c1, c2):
        a1, b1 = c1
        a2, b2 = c2
        return a1 * a2, a2 * b1 + b2
    a_cum, b_cum = lax.associative_scan(combine, (a, u), axis=1)
    return a_cum * h0[:, None] + b_cum


def rglru_mixer(x, conv_buf, h0, w_in, conv_w, conv_b, w_gate, b_gate, lam, w_out):
    B, L, _ = x.shape
    gate_br, xb = jnp.split(x @ w_in, 2, axis=-1)
    xpad = jnp.concatenate([conv_buf.astype(xb.dtype), xb], axis=1)
    xc = conv_b + sum(conv_w[j] * xpad[:, j:j + L] for j in range(CONV_W))
    new_buf = xpad[:, L:]
    gates = jnp.einsum('blnc,ncd->blnd', xc.reshape(B, L, N_BLK_B, BW_B), w_gate)
    r = jax.nn.sigmoid((gates[..., :BW_B].reshape(B, L, D_RNN) + b_gate[0]).astype(F32))
    ig = jax.nn.sigmoid((gates[..., BW_B:].reshape(B, L, D_RNN) + b_gate[1]).astype(F32))
    log_a = LRU_C * r * jax.nn.log_sigmoid(lam.astype(F32))
    u = jnp.sqrt(-jnp.expm1(2.0 * log_a)) * (ig * xc.astype(F32))
    h = linear_scan(jnp.exp(log_a), u, h0.astype(F32))
    y = (h.astype(x.dtype) * jax.nn.gelu(gate_br)) @ w_out
    return y, new_buf, h[:, -1]


def rope(x, pos):
    half = x.shape[-1] // 2
    inv = ROPE_BASE ** (-jnp.arange(half, dtype=F32) / half)
    ang = pos.astype(F32)[:, None] * inv[None, :]
    cos, sin = jnp.cos(ang)[None, :, None, :], jnp.sin(ang)[None, :, None, :]
    x1, x2 = x[..., :half].astype(F32), x[..., half:].astype(F32)
    return jnp.concatenate([x1 * cos - x2 * sin, x1 * sin + x2 * cos], axis=-1)


def retention_chunk(S, q, k, v, log_g):
    T = q.shape[1]
    idx = jnp.arange(T, dtype=F32)
    diff = idx[:, None] - idx[None, :]
    decay = jnp.where(diff >= 0, jnp.exp(jnp.maximum(diff, 0.0) * log_g[:, None, None]), 0.0)
    scores = jnp.einsum('bthd,bshd->bhts', q, k) * decay
    o = jnp.einsum('bhts,bshe->bthe', scores, v)
    xi = jnp.exp((idx + 1.0)[None, :] * log_g[:, None])
    o = o + jnp.einsum('bthd,bhde->bthe', q, S) * xi.T[None, :, :, None]
    zeta = jnp.exp((T - 1.0 - idx)[None, :] * log_g[:, None])
    S_new = jnp.exp(T * log_g)[None, :, None, None] * S + jnp.einsum('bshd,bshe->bhde', k * zeta.T[None, :, :, None], v)
    return S_new, o


def retention_mixer(x, S0, pos, w_in, gn_g, gn_b, w_out):
    B, L, _ = x.shape
    qk_w, v_w = H_C * DK_C, H_C * DV_C
    q, k, v, g = jnp.split(x @ w_in, [qk_w, 2 * qk_w, 2 * qk_w + v_w], axis=-1)
    q = rope(q.reshape(B, L, H_C, DK_C), pos)
    k = rope(k.reshape(B, L, H_C, DK_C), pos) * DK_C ** -0.5
    v = v.reshape(B, L, H_C, DV_C).astype(F32)
    log_g = jnp.log1p(-jnp.exp2(-5.0 - jnp.arange(H_C, dtype=F32)))
    chunk = CHUNK_C if L % CHUNK_C == 0 else L
    n_chunks = L // chunk

    def to_chunks(t):
        return t.reshape(B, n_chunks, chunk, *t.shape[2:]).swapaxes(0, 1)

    def step(S, c):
        return retention_chunk(S, c[0], c[1], c[2], log_g)

    S, o = lax.scan(step, S0.astype(F32), (to_chunks(q), to_chunks(k), to_chunks(v)))
    o = o.swapaxes(0, 1).reshape(B, L, H_C, DV_C)
    mu = jnp.mean(o, -1, keepdims=True)
    var = jnp.mean(jnp.square(o - mu), -1, keepdims=True)
    o = ((o - mu) * lax.rsqrt(var + GN_EPS)).reshape(B, L, v_w) * gn_g + gn_b
    y = (jax.nn.silu(g.astype(F32)) * o).astype(x.dtype) @ w_out
    return y, S


def setup_inputs(seed: int = 0) -> dict:
    key = jax.random.key(seed)
    ks = jax.random.split(key, 32)
    n_pages = PAST_LEN // PAGE_SIZE
    n_used = DEC_BATCH * n_pages
    n_pool = n_used + max(1, n_used // 4)
    nrm = jax.random.normal
    page_table = jax.random.permutation(ks[0], n_pool)[:n_used].reshape(DEC_BATCH, n_pages).astype(jnp.int32)
    a_c = jax.random.uniform(ks[1], (N_LAYERS_B, D_RNN), minval=0.9, maxval=0.999)
    a_base = a_c ** (1.0 / LRU_C)
    lam_b = jnp.log(a_base) - jnp.log1p(-a_base)
    forget_bias = jnp.linspace(FORGET_BIAS_MIN, FORGET_BIAS_MAX, H_A, dtype=F32)
    return {
        "x_prompt": nrm(ks[2], (BATCH, SEQ, D_MODEL), F32),
        "x_sample": nrm(ks[3], (DEC_BATCH, DEC_SEQ, D_MODEL), F32),
        "cache_k_a": nrm(ks[4], (N_LAYERS_A, n_pool, PAGE_SIZE, H_A, DH_A), F32),
        "cache_v_a": nrm(ks[5], (N_LAYERS_A, n_pool, PAGE_SIZE, H_A, DH_A), F32),
        "cache_logf_a": jax.nn.log_sigmoid(forget_bias + nrm(ks[6], (N_LAYERS_A, n_pool, PAGE_SIZE, H_A), F32)),
        "page_table": page_table,
        "state_conv_b": nrm(ks[7], (N_LAYERS_B, DEC_BATCH, CONV_W - 1, D_RNN), F32),
        "state_h_b": 0.5 * nrm(ks[8], (N_LAYERS_B, DEC_BATCH, D_RNN), F32),
        "state_ret_c": 0.1 * nrm(ks[9], (N_LAYERS_C, DEC_BATCH, H_C, DK_C, DV_C), F32),
        "w_in_a": nrm(ks[10], (N_LAYERS_A, D_MODEL, 3 * D_MODEL + H_A), F32) * D_MODEL ** -0.5,
        "b_f_a": forget_bias + 0.1 * nrm(ks[11], (N_LAYERS_A, H_A), F32),
        "w_out_a": nrm(ks[12], (N_LAYERS_A, D_MODEL, D_MODEL), F32) * D_MODEL ** -0.5 * BETA,
        "w_in_b": nrm(ks[13], (N_LAYERS_B, D_MODEL, 2 * D_RNN), F32) * D_MODEL ** -0.5,
        "conv_w_b": nrm(ks[14], (N_LAYERS_B, CONV_W, D_RNN), F32) * CONV_W ** -0.5,
        "conv_b_b": 0.02 * nrm(ks[15], (N_LAYERS_B, D_RNN), F32),
        "w_gate_b": nrm(ks[16], (N_LAYERS_B, N_BLK_B, BW_B, 2 * BW_B), F32) * BW_B ** -0.5,
        "b_gate_b": 0.02 * nrm(ks[17], (N_LAYERS_B, 2, D_RNN), F32),
        "lam_b": lam_b,
        "w_out_b": nrm(ks[18], (N_LAYERS_B, D_RNN, D_MODEL), F32) * D_RNN ** -0.5 * BETA,
        "w_in_c": nrm(ks[19], (N_LAYERS_C, D_MODEL, 2 * H_C * DK_C + 2 * H_C * DV_C), F32) * D_MODEL ** -0.5,
        "gn_g_c": 1.0 + 0.02 * nrm(ks[20], (N_LAYERS_C, H_C * DV_C), F32),
        "gn_b_c": 0.02 * nrm(ks[21], (N_LAYERS_C, H_C * DV_C), F32),
        "w_out_c": nrm(ks[22], (N_LAYERS_C, H_C * DV_C, D_MODEL), F32) * (H_C * DV_C) ** -0.5 * BETA,
        "ln1_g": 1.0 + 0.02 * nrm(ks[23], (DEPTH, D_MODEL), F32),
        "ln1_b": 0.02 * nrm(ks[24], (DEPTH, D_MODEL), F32),
        "w_up_ffn": nrm(ks[25], (DEPTH, D_MODEL, 2 * D_FF), F32) * D_MODEL ** -0.5,
        "w_down_ffn": nrm(ks[26], (DEPTH, D_FF, D_MODEL), F32) * D_FF ** -0.5 * BETA,
        "ln2_g": 1.0 + 0.02 * nrm(ks[27], (DEPTH, D_MODEL), F32),
        "ln2_b": 0.02 * nrm(ks[28], (DEPTH, D_MODEL), F32),
    }


def reference(x_prompt, x_sample, cache_k_a, cache_v_a, cache_logf_a, page_table,
              state_conv_b, state_h_b, state_ret_c,
              w_in_a, b_f_a, w_out_a,
              w_in_b, conv_w_b, conv_b_b, w_gate_b, b_gate_b, lam_b, w_out_b,
              w_in_c, gn_g_c, gn_b_c, w_out_c,
              ln1_g, ln1_b, w_up_ffn, w_down_ffn, ln2_g, ln2_b):
    xp, xs = x_prompt, x_sample
    Bp, Lp = xp.shape[:2]
    pos_p = jnp.arange(Lp)
    pos_s = PAST_LEN + jnp.arange(xs.shape[1])
    kpa, vpa, lfpa, ksa, vsa, lfsa = [], [], [], [], [], []
    cpb, hpb, csb, hsb, rpc, rsc = [], [], [], [], [], []
    for i in range(DEPTH):
        j = i // N_MIXERS
        kind = i % N_MIXERS
        if kind == 0:
            qp, kp, vp, lfp = fox_project(xp, w_in_a[j], b_f_a[j])
            mp = fox_prompt(qp, kp, vp, lfp) @ w_out_a[j]
            qs, ks_, vs, lfs = fox_project(xs, w_in_a[j], b_f_a[j])
            ms = fox_sample(qs, ks_, vs, lfs, cache_k_a[j], cache_v_a[j], cache_logf_a[j], page_table) @ w_out_a[j]
            kpa.append(kp); vpa.append(vp); lfpa.append(lfp)
            ksa.append(ks_); vsa.append(vs); lfsa.append(lfs)
        elif kind == 1:
            zero_buf = jnp.zeros((Bp, CONV_W - 1, D_RNN), xp.dtype)
            zero_h = jnp.zeros((Bp, D_RNN), F32)
            mp, buf_p, h_p = rglru_mixer(xp, zero_buf, zero_h, w_in_b[j], conv_w_b[j], conv_b_b[j],
                                         w_gate_b[j], b_gate_b[j], lam_b[j], w_out_b[j])
            ms, buf_s, h_s = rglru_mixer(xs, state_conv_b[j], state_h_b[j], w_in_b[j], conv_w_b[j], conv_b_b[j],
                                         w_gate_b[j], b_gate_b[j], lam_b[j], w_out_b[j])
            cpb.append(buf_p); hpb.append(h_p); csb.append(buf_s); hsb.append(h_s)
        else:
            zero_S = jnp.zeros((Bp, H_C, DK_C, DV_C), F32)
            mp, S_p = retention_mixer(xp, zero_S, pos_p, w_in_c[j], gn_g_c[j], gn_b_c[j], w_out_c[j])
            ms, S_s = retention_mixer(xs, state_ret_c[j], pos_s, w_in_c[j], gn_g_c[j], gn_b_c[j], w_out_c[j])
            rpc.append(S_p); rsc.append(S_s)
        xp = layer_norm(ALPHA * xp + mp.astype(xp.dtype), ln1_g[i], ln1_b[i])
        xs = layer_norm(ALPHA * xs + ms.astype(xs.dtype), ln1_g[i], ln1_b[i])
        xp = layer_norm(ALPHA * xp + swiglu_ffn(xp, w_up_ffn[i], w_down_ffn[i]), ln2_g[i], ln2_b[i])
        xs = layer_norm(ALPHA * xs + swiglu_ffn(xs, w_up_ffn[i], w_down_ffn[i]), ln2_g[i], ln2_b[i])
    new_k_prompt_a = jnp.stack(kpa).astype(cache_k_a.dtype)
    new_v_prompt_a = jnp.stack(vpa).astype(cache_v_a.dtype)
    new_logf_prompt_a = jnp.stack(lfpa).astype(cache_logf_a.dtype)
    new_k_sample_a = jnp.stack(ksa).astype(cache_k_a.dtype)
    new_v_sample_a = jnp.stack(vsa).astype(cache_v_a.dtype)
    new_logf_sample_a = jnp.stack(lfsa).astype(cache_logf_a.dtype)
    new_conv_prompt_b = jnp.stack(cpb).astype(state_conv_b.dtype)
    new_h_prompt_b = jnp.stack(hpb).astype(state_h_b.dtype)
    new_conv_sample_b = jnp.stack(csb).astype(state_conv_b.dtype)
    new_h_sample_b = jnp.stack(hsb).astype(state_h_b.dtype)
    new_ret_prompt_c = jnp.stack(rpc).astype(state_ret_c.dtype)
    new_ret_sample_c = jnp.stack(rsc).astype(state_ret_c.dtype)
    return (xp, xs, new_k_prompt_a, new_v_prompt_a, new_logf_prompt_a,
            new_k_sample_a, new_v_sample_a, new_logf_sample_a,
            new_conv_prompt_b, new_h_prompt_b, new_conv_sample_b, new_h_sample_b,
            new_ret_prompt_c, new_ret_sample_c)
```

```python
import functools
import math

import jax
import jax.numpy as jnp
from jax import lax
from jax.experimental import pallas as pl
from jax.experimental.pallas import tpu as pltpu

F32 = jnp.float32
BF16 = jnp.bfloat16

D_MODEL = 1024
DEPTH = 4
H_A = 16
DH_A = D_MODEL // H_A
D_RNN = 1408
N_BLK_B = 16
BW_B = D_RNN // N_BLK_B
CONV_W = 4
LRU_C = 8.0
H_C = 4
DK_C = D_MODEL // H_C
DV_C = 2 * DK_C
CHUNK_C = 128
ROPE_BASE = 10000.0
D_FF = 2816
ALPHA = (2 * DEPTH) ** 0.25
LN_EPS = 1e-5
GN_EPS = 1e-5

LANES = 128
SUBLANES = 8
MXU_COLS = 256
VMEM_LIMIT = 56 * 1024 * 1024

_NT = (((1,), (1,)), ((), ()))
_TN = (((0,), (0,)), ((), ()))


def _params(*sem):
    return pltpu.CompilerParams(dimension_semantics=sem, vmem_limit_bytes=VMEM_LIMIT)


def _const_spec(shape):
    zeros = (0,) * len(shape)
    return pl.BlockSpec(shape, lambda *_: zeros, pipeline_mode=pl.Buffered(1))


def _dot(a, b):
    return jnp.dot(a, b, preferred_element_type=F32)


def _layer_norm(z, g, b):
    mu = jnp.mean(z, axis=-1, keepdims=True)
    zc = z - mu
    var = jnp.mean(zc * zc, axis=-1, keepdims=True)
    return zc * lax.rsqrt(var + LN_EPS) * g + b


def _log_sigmoid(z):
    return jnp.minimum(z, 0.0) - jnp.log1p(jnp.exp(-jnp.abs(z)))


def _dot_f32_by_01(x, m01):
    hi = x.astype(BF16)
    r1 = x - hi.astype(F32)
    mid = r1.astype(BF16)
    lo = (r1 - mid.astype(F32)).astype(BF16)
    return _dot(hi, m01) + _dot(mid, m01) + _dot(lo, m01)


def _tri01(n, strict_lower_rows):
    r = lax.broadcasted_iota(jnp.int32, (n, n), 0)
    c = lax.broadcasted_iota(jnp.int32, (n, n), 1)
    keep = (r > c) if strict_lower_rows else (r <= c)
    return jnp.where(keep, 1.0, 0.0).astype(BF16)


def _proj_ln_kernel(a_ref, w_ref, x_ref, g_ref, b_ref, y_ref, ybf_ref):
    m = _dot(a_ref[...], w_ref[...])
    y = _layer_norm(ALPHA * x_ref[...] + m, g_ref[...], b_ref[...])
    y_ref[...] = y
    ybf_ref[...] = y.astype(BF16)


def proj_ln(a_bf, w_bf, x, g, b, tm):
    m, k = a_bf.shape
    d = w_bf.shape[1]
    tm = min(tm, m)
    row = lambda i: (i, 0)
    return pl.pallas_call(
        _proj_ln_kernel,
        grid=(m // tm,),
        in_specs=[pl.BlockSpec((tm, k), row), _const_spec((k, d)), pl.BlockSpec((tm, d), row),
                  _const_spec((1, d)), _const_spec((1, d))],
        out_specs=[pl.BlockSpec((tm, d), row), pl.BlockSpec((tm, d), row)],
        out_shape=[jax.ShapeDtypeStruct((m, d), F32), jax.ShapeDtypeStruct((m, d), BF16)],
        compiler_params=_params("parallel"),
        name="proj_ln",
    )(a_bf, w_bf, x, g.reshape(1, d), b.reshape(1, d))


def _ffn_ln_kernel(xbf_ref, x_ref, wup_ref, wdn_ref, g_ref, b_ref, y_ref, ybf_ref, *, n_chunks):
    xb = xbf_ref[...]
    cw = D_FF // n_chunks
    acc = None
    for c in range(n_chunks):
        gate = _dot(xb, wup_ref[:, c * cw:(c + 1) * cw])
        up = _dot(xb, wup_ref[:, D_FF + c * cw:D_FF + (c + 1) * cw])
        h = (jax.nn.silu(gate) * up).astype(BF16)
        part = _dot(h, wdn_ref[c * cw:(c + 1) * cw, :])
        acc = part if acc is None else acc + part
    y = _layer_norm(ALPHA * x_ref[...] + acc, g_ref[...], b_ref[...])
    y_ref[...] = y
    ybf_ref[...] = y.astype(BF16)


def ffn_ln(x_bf, x, w_up_bf, w_dn_bf, g, b, tm, n_chunks=2):
    m, d = x.shape
    tm = min(tm, m)
    row = lambda i: (i, 0)
    return pl.pallas_call(
        functools.partial(_ffn_ln_kernel, n_chunks=n_chunks),
        grid=(m // tm,),
        in_specs=[pl.BlockSpec((tm, d), row), pl.BlockSpec((tm, d), row),
                  _const_spec((d, 2 * D_FF)), _const_spec((D_FF, d)),
                  _const_spec((1, d)), _const_spec((1, d))],
        out_specs=[pl.BlockSpec((tm, d), row), pl.BlockSpec((tm, d), row)],
        out_shape=[jax.ShapeDtypeStruct((m, d), F32), jax.ShapeDtypeStruct((m, d), BF16)],
        compiler_params=_params("parallel"),
        name="ffn_ln",
    )(x_bf, x, w_up_bf, w_dn_bf, g.reshape(1, d), b.reshape(1, d))


def _fox_proj_kernel(x_ref, wq_ref, wkT_ref, wvT_ref, wfT_ref, bf_ref,
                     q_ref, kT_ref, vT_ref, kTb_ref, vTb_ref, lfT_ref):
    x = x_ref[0]
    q_ref[0] = (_dot(x, wq_ref[...]) * DH_A ** -0.5).astype(BF16)
    kT = lax.dot_general(wkT_ref[...], x, _NT, preferred_element_type=F32)
    kT_ref[0] = kT
    kTb_ref[0] = kT.astype(BF16)
    vT = lax.dot_general(wvT_ref[...], x, _NT, preferred_element_type=F32)
    vT_ref[0] = vT
    vTb_ref[0] = vT.astype(BF16)
    f = lax.dot_general(wfT_ref[...], x, _NT, preferred_element_type=F32)
    lfT_ref[0] = _log_sigmoid(f + bf_ref[...])


def fox_proj(x_bf, wq, wkT, wvT, wfT, bf_col, tm):
    bsz, seq, d = x_bf.shape
    xmap = lambda b, i: (b, i, 0)
    tmap = lambda b, i: (b, 0, i)
    return pl.pallas_call(
        _fox_proj_kernel,
        grid=(bsz, seq // tm),
        in_specs=[pl.BlockSpec((1, tm, d), xmap), _const_spec((d, d)), _const_spec((d, d)),
                  _const_spec((d, d)), _const_spec((H_A, d)), _const_spec((H_A, 1))],
        out_specs=[pl.BlockSpec((1, tm, d), xmap), pl.BlockSpec((1, d, tm), tmap),
                   pl.BlockSpec((1, d, tm), tmap), pl.BlockSpec((1, d, tm), tmap),
                   pl.BlockSpec((1, d, tm), tmap), pl.BlockSpec((1, H_A, tm), tmap)],
        out_shape=[jax.ShapeDtypeStruct((bsz, seq, d), BF16),
                   jax.ShapeDtypeStruct((bsz, d, seq), F32), jax.ShapeDtypeStruct((bsz, d, seq), F32),
                   jax.ShapeDtypeStruct((bsz, d, seq), BF16), jax.ShapeDtypeStruct((bsz, d, seq), BF16),
                   jax.ShapeDtypeStruct((bsz, H_A, seq), F32)],
        compiler_params=_params("parallel", "parallel"),
        name="fox_proj",
    )(x_bf, wq, wkT, wvT, wfT, bf_col)


def _fox_cumsum_kernel(lf_ref, ct_ref, *, chunk):
    seq = lf_ref.shape[2]
    prefix = _tri01(chunk, strict_lower_rows=False)
    carry = jnp.zeros((H_A, 1), F32)
    for i in range(seq // chunk):
        cs = _dot_f32_by_01(lf_ref[0, :, i * chunk:(i + 1) * chunk], prefix) + carry
        ct_ref[0, :, i * chunk:(i + 1) * chunk] = cs
        carry = cs[:, chunk - 1:chunk]


def fox_cumsum(lfT):
    bsz, h, seq = lfT.shape
    chunk = min(seq, MXU_COLS)
    spec = pl.BlockSpec((1, h, seq), lambda b: (b, 0, 0))
    return pl.pallas_call(
        functools.partial(_fox_cumsum_kernel, chunk=chunk),
        grid=(bsz,), in_specs=[spec], out_specs=spec,
        out_shape=jax.ShapeDtypeStruct((bsz, h, seq), F32),
        compiler_params=_params("parallel"),
        name="fox_cumsum",
    )(lfT)


def _fox_attn_kernel(q_ref, kT_ref, vT_ref, c_ref, ct_ref, o_ref, cq_sc, m_sc, l_sc, acc_sc, *, tq):
    p = pl.program_id(1)
    i = pl.program_id(2)
    j = pl.program_id(3)
    lane = lax.broadcasted_iota(jnp.int32, (tq, LANES), 1)
    first_head = lane < DH_A

    @pl.when(j == 0)
    def _init():
        cblk = c_ref[0]
        hl = lax.broadcasted_iota(jnp.int32, cblk.shape, 1)
        for a in range(2):
            cq_sc[a] = jnp.sum(jnp.where(hl == 2 * p + a, cblk, 0.0), axis=1, keepdims=True)
        m_sc[...] = jnp.full(m_sc.shape, -jnp.inf, F32)
        l_sc[...] = jnp.zeros(l_sc.shape, F32)
        acc_sc[...] = jnp.zeros(acc_sc.shape, F32)

    def step(diagonal):
        q = q_ref[0]
        kT = kT_ref[0]
        vT = vT_ref[0]
        alphas, pvs = [], []
        for a in range(2):
            qa = jnp.where(first_head if a == 0 else jnp.logical_not(first_head), q, jnp.zeros_like(q))
            s = _dot(qa, kT)
            ck = ct_ref[0, pl.ds(2 * p + a, 1), :]
            s = s + (cq_sc[a] - ck)
            if diagonal:
                r = lax.broadcasted_iota(jnp.int32, s.shape, 0)
                c = lax.broadcasted_iota(jnp.int32, s.shape, 1)
                s = jnp.where(r >= c, s, -jnp.inf)
            m_old = m_sc[a]
            m_new = jnp.maximum(m_old, jnp.max(s, axis=1, keepdims=True))
            alpha = jnp.exp(m_old - m_new)
            pr = jnp.exp(s - m_new)
            l_sc[a] = alpha * l_sc[a] + jnp.sum(pr, axis=1, keepdims=True)
            m_sc[a] = m_new
            pvs.append(lax.dot_general(pr.astype(BF16), vT, _NT, preferred_element_type=F32))
            alphas.append(alpha)
        acc_sc[...] = (acc_sc[...] * jnp.where(first_head, alphas[0], alphas[1])
                       + jnp.where(first_head, pvs[0], pvs[1]))

    @pl.when(j < i)
    def _below():
        step(False)

    @pl.when(j == i)
    def _diag():
        step(True)
        o_ref[0] = (acc_sc[...] / jnp.where(first_head, l_sc[0], l_sc[1])).astype(BF16)


def fox_attn(q_bf, kT_bf, vT_bf, c, ct, tq):
    bsz, seq, d = q_bf.shape
    nq = seq // tq
    qmap = lambda b, p, i, j: (b, i, p)
    kmap = lambda b, p, i, j: (b, p, jnp.minimum(i, j))
    return pl.pallas_call(
        functools.partial(_fox_attn_kernel, tq=tq),
        grid=(bsz, H_A // 2, nq, nq),
        in_specs=[pl.BlockSpec((1, tq, LANES), qmap), pl.BlockSpec((1, LANES, tq), kmap),
                  pl.BlockSpec((1, LANES, tq), kmap),
                  pl.BlockSpec((1, tq, H_A), lambda b, p, i, j: (b, i, 0)),
                  pl.BlockSpec((1, H_A, tq), lambda b, p, i, j: (b, 0, jnp.minimum(i, j)))],
        out_specs=pl.BlockSpec((1, tq, LANES), qmap),
        out_shape=jax.ShapeDtypeStruct((bsz, seq, d), BF16),
        scratch_shapes=[pltpu.VMEM((2, tq, 1), F32), pltpu.VMEM((2, tq, 1), F32),
                        pltpu.VMEM((2, tq, 1), F32), pltpu.VMEM((tq, LANES), F32)],
        compiler_params=_params("parallel", "parallel", "parallel", "arbitrary"),
        name="fox_attn",
    )(q_bf, kT_bf, vT_bf, c, ct)


def _fox_proj_sample_kernel(x_ref, w_ref, wf_ref, bf_ref, q_ref, k_ref, v_ref, lf_ref):
    x = x_ref[...]
    d = x.shape[1]
    q_ref[...] = _dot(x, w_ref[:, 0:d])
    k_ref[...] = _dot(x, w_ref[:, d:2 * d])
    v_ref[...] = _dot(x, w_ref[:, 2 * d:3 * d])
    lf_ref[...] = _log_sigmoid(_dot(x, wf_ref[...]) + bf_ref[...])


def fox_proj_sample(x_bf, w_qkv, wf_pad, bf_pad):
    m, d = x_bf.shape
    full = lambda shape: pl.BlockSpec(shape, lambda: (0,) * len(shape))
    return pl.pallas_call(
        _fox_proj_sample_kernel,
        in_specs=[full((m, d)), full((d, 3 * d)), full((d, LANES)), full((1, LANES))],
        out_specs=[full((m, d)), full((m, d)), full((m, d)), full((m, LANES))],
        out_shape=[jax.ShapeDtypeStruct((m, d), F32)] * 3 + [jax.ShapeDtypeStruct((m, LANES), F32)],
        compiler_params=pltpu.CompilerParams(vmem_limit_bytes=VMEM_LIMIT),
        name="fox_proj_sample",
    )(x_bf, w_qkv, wf_pad, bf_pad)


def _fox_sample_attn_kernel(pt_ref, kT_ref, vT_ref, lfT_ref, q_ref, kn_ref, vn_ref, lfs_ref, o_ref,
                            qbd_sc, ctn_sc, ctcol_sc, m_sc, l_sc, acc_sc, carry_sc, *, n_new):
    j = pl.program_id(1)
    rows = n_new * H_A
    d = qbd_sc.shape[1]
    page = kT_ref.shape[1]

    def head_diag():
        r = lax.broadcasted_iota(jnp.int32, (rows, d), 0)
        c = lax.broadcasted_iota(jnp.int32, (rows, d), 1)
        return (c // DH_A) == (r % H_A)

    def online_update(s, v_fn):
        m_old = m_sc[...]
        m_new = jnp.maximum(m_old, jnp.max(s, axis=1, keepdims=True))
        alpha = jnp.exp(m_old - m_new)
        pr = jnp.exp(s - m_new)
        l_sc[...] = alpha * l_sc[...] + jnp.sum(pr, axis=1, keepdims=True)
        m_sc[...] = m_new
        acc_sc[...] = alpha * acc_sc[...] + v_fn(pr.astype(BF16))

    @pl.when(j == 0)
    def _init():
        q = q_ref[0]
        qe = jnp.concatenate([jnp.broadcast_to(q[t:t + 1, :], (H_A, d)) for t in range(n_new)], axis=0)
        qbd_sc[...] = jnp.where(head_diag(), qe * DH_A ** -0.5, 0.0).astype(BF16)
        ctn = _dot_f32_by_01(lfs_ref[0], _tri01(LANES, strict_lower_rows=False))
        ctn_sc[...] = ctn
        ctcol_sc[...] = jnp.concatenate([ctn[:, t:t + 1] for t in range(n_new)], axis=0)
        m_sc[...] = jnp.full(m_sc.shape, -jnp.inf, F32)
        l_sc[...] = jnp.zeros(l_sc.shape, F32)
        acc_sc[...] = jnp.zeros(acc_sc.shape, F32)
        carry_sc[...] = jnp.zeros(carry_sc.shape, F32)

    s = _dot(qbd_sc[...], kT_ref[...].astype(BF16))
    lf = lfT_ref[...]
    suffix = _dot_f32_by_01(lf, _tri01(page, strict_lower_rows=True)) + carry_sc[...]
    carry_sc[...] = carry_sc[...] + jnp.sum(lf, axis=1, keepdims=True)
    s = s + jnp.concatenate([suffix] * n_new, axis=0) + ctcol_sc[...]
    online_update(s, lambda p: lax.dot_general(p, vT_ref[...].astype(BF16), _NT, preferred_element_type=F32))

    @pl.when(j == pl.num_programs(1) - 1)
    def _finish():
        pad = jnp.zeros((SUBLANES - n_new, d), F32)
        kn = jnp.concatenate([kn_ref[0], pad], axis=0).astype(BF16)
        vn = jnp.concatenate([vn_ref[0], pad], axis=0).astype(BF16)
        sn = lax.dot_general(qbd_sc[...], kn, _NT, preferred_element_type=F32)
        ctk = jnp.concatenate([ctn_sc[:, 0:SUBLANES]] * n_new, axis=0)
        tq = lax.broadcasted_iota(jnp.int32, sn.shape, 0) // H_A
        tk = lax.broadcasted_iota(jnp.int32, sn.shape, 1)
        sn = jnp.where(tk <= tq, sn + (ctcol_sc[...] - ctk), -jnp.inf)
        online_update(sn, lambda p: _dot(p, vn))
        om = jnp.where(head_diag(), acc_sc[...] / l_sc[...], 0.0)
        o_ref[0] = jnp.concatenate(
            [jnp.sum(om[t * H_A:(t + 1) * H_A], axis=0, keepdims=True) for t in range(n_new)], axis=0)


def fox_sample_attn(page_table, kT_pool, vT_pool, lfT_pool, layer, q, k_new, v_new, lfs_T):
    bd, n_new, d = q.shape
    n_pages = page_table.shape[1]
    page = kT_pool.shape[3]
    rows = n_new * H_A
    pmap = lambda b, j, pt: (layer, pt[b, n_pages - 1 - j], 0, 0)
    bmap = lambda b, j, pt: (b, 0, 0)
    grid_spec = pltpu.PrefetchScalarGridSpec(
        num_scalar_prefetch=1,
        grid=(bd, n_pages),
        in_specs=[pl.BlockSpec((None, None, d, page), pmap), pl.BlockSpec((None, None, d, page), pmap),
                  pl.BlockSpec((None, None, H_A, page), pmap),
                  pl.BlockSpec((1, n_new, d), bmap), pl.BlockSpec((1, n_new, d), bmap),
                  pl.BlockSpec((1, n_new, d), bmap), pl.BlockSpec((1, H_A, LANES), bmap)],
        out_specs=pl.BlockSpec((1, n_new, d), bmap),
        scratch_shapes=[pltpu.VMEM((rows, d), BF16), pltpu.VMEM((H_A, LANES), F32),
                        pltpu.VMEM((rows, 1), F32), pltpu.VMEM((rows, 1), F32), pltpu.VMEM((rows, 1), F32),
                        pltpu.VMEM((rows, d), F32), pltpu.VMEM((H_A, 1), F32)],
    )
    return pl.pallas_call(
        functools.partial(_fox_sample_attn_kernel, n_new=n_new),
        grid_spec=grid_spec,
        out_shape=jax.ShapeDtypeStruct((bd, n_new, d), F32),
        compiler_params=_params("parallel", "arbitrary"),
        name="fox_sample_attn",
    )(page_table, kT_pool, vT_pool, lfT_pool, q, k_new, v_new, lfs_T)


GATE_TILE = MXU_COLS
GATE_WIN = 2 * MXU_COLS
N_GATE_TILES = -(-D_RNN // GATE_TILE)


def _gate_window_start(j):
    first_block = (j * GATE_TILE) // BW_B
    return min((first_block * BW_B) // LANES * LANES, D_RNN - GATE_WIN)


def _band_gate_weights(w_gate):
    eye = jnp.eye(N_BLK_B, dtype=w_gate.dtype)
    tiles = []
    for part in range(2):
        wp = w_gate[:, :, part * BW_B:(part + 1) * BW_B]
        dense = (eye[:, None, :, None] * wp[:, :, None, :]).reshape(D_RNN, D_RNN)
        dense = jnp.pad(dense, ((0, 0), (0, N_GATE_TILES * GATE_TILE - D_RNN)))
        tiles.append(jnp.stack([
            dense[_gate_window_start(j):_gate_window_start(j) + GATE_WIN, j * GATE_TILE:(j + 1) * GATE_TILE]
            for j in range(N_GATE_TILES)]))
    return jnp.stack(tiles).astype(BF16)


def _rglru_gate_tile(j, xc, xc_bf, wband_ref, bg_ref, lam_ref):
    lo = j * GATE_TILE
    w = min(GATE_TILE, D_RNN - lo)
    ks = _gate_window_start(j)
    xw = xc_bf[:, ks:ks + GATE_WIN]
    r = jax.nn.sigmoid(_dot(xw, wband_ref[0, j])[:, :w] + bg_ref[0:1, lo:lo + w])
    ig = jax.nn.sigmoid(_dot(xw, wband_ref[1, j])[:, :w] + bg_ref[1:2, lo:lo + w])
    log_a = LRU_C * r * _log_sigmoid(lam_ref[:, lo:lo + w])
    a = jnp.exp(log_a)
    u = jnp.sqrt(-jnp.tanh(log_a) * (a * a + 1.0)) * (ig * xc[:, lo:lo + w])
    return a, u


def _rglru_prompt_kernel(xbf_ref, x_ref, win_ref, cw_ref, cb_ref, wband_ref, bg_ref, lam_ref, wout_ref,
                         g_ref, b_ref, y_ref, ybf_ref, tail_ref, hlast_ref, tail_sc, h_sc, a_sc, u_sc):
    t_rows = xbf_ref.shape[1]

    @pl.when(pl.program_id(1) == 0)
    def _reset():
        tail_sc[...] = jnp.zeros(tail_sc.shape, F32)
        h_sc[...] = jnp.zeros(h_sc.shape, F32)

    xb16 = xbf_ref[0]
    gate_br = _dot(xb16, win_ref[:, 0:D_RNN])
    xb = _dot(xb16, win_ref[:, D_RNN:2 * D_RNN])

    row8 = lax.broadcasted_iota(jnp.int32, (SUBLANES, D_RNN), 0)
    tail = tail_sc[...]
    xc = cb_ref[...] + cw_ref[CONV_W - 1:CONV_W, :] * xb
    for k in range(1, CONV_W):
        xs = pltpu.roll(xb, k, axis=0)
        head = jnp.where(row8 < k, pltpu.roll(tail, k, axis=0), xs[0:SUBLANES])
        xs = jnp.concatenate([head, xs[SUBLANES:]], axis=0)
        xc = xc + cw_ref[CONV_W - 1 - k:CONV_W - k, :] * xs
    new_tail = xb[t_rows - SUBLANES:t_rows]
    tail_sc[...] = new_tail
    tail_ref[0] = new_tail
    xc_bf = xc.astype(BF16)

    for j in range(N_GATE_TILES):
        lo = j * GATE_TILE
        w = min(GATE_TILE, D_RNN - lo)
        a, u = _rglru_gate_tile(j, xc, xc_bf, wband_ref, bg_ref, lam_ref)
        pos = lax.broadcasted_iota(jnp.int32, a.shape, 0) % SUBLANES
        for s in (1, 2, 4):
            live = pos >= s
            u = jnp.where(live, a * pltpu.roll(u, s, axis=0) + u, u)
            a = jnp.where(live, a * pltpu.roll(a, s, axis=0), a)
        a_sc[:, lo:lo + w] = a
        u_sc[:, lo:lo + w] = u

        def group(i, h_prev, lo=lo, w=w):
            blk = pl.ds(pl.multiple_of(i * SUBLANES, SUBLANES), SUBLANES)
            hb = a_sc[blk, lo:lo + w] * h_prev + u_sc[blk, lo:lo + w]
            u_sc[blk, lo:lo + w] = hb
            return hb[SUBLANES - 1:SUBLANES, :]

        h_sc[:, lo:lo + w] = lax.fori_loop(0, t_rows // SUBLANES, group, h_sc[:, lo:lo + w])

    hlast_ref[0] = jnp.broadcast_to(h_sc[...], (SUBLANES, D_RNN))
    mixed = (u_sc[...] * jax.nn.gelu(gate_br)).astype(BF16)
    y = _layer_norm(ALPHA * x_ref[0] + _dot(mixed, wout_ref[...]), g_ref[...], b_ref[...])
    y_ref[0] = y
    ybf_ref[0] = y.astype(BF16)


def rglru_prompt(x_bf, x, w_in, conv_w, conv_b, wband, b_gate, lam, w_out, g, b, t_rows):
    bsz, seq, d = x.shape
    xmap = lambda bi, ti: (bi, ti, 0)
    smap = lambda bi, ti: (bi, 0, 0)
    return pl.pallas_call(
        _rglru_prompt_kernel,
        grid=(bsz, seq // t_rows),
        in_specs=[pl.BlockSpec((1, t_rows, d), xmap), pl.BlockSpec((1, t_rows, d), xmap),
                  _const_spec((d, 2 * D_RNN)), _const_spec((CONV_W, D_RNN)), _const_spec((1, D_RNN)),
                  _const_spec((2, N_GATE_TILES, GATE_WIN, GATE_TILE)), _const_spec((2, D_RNN)),
                  _const_spec((1, D_RNN)), _const_spec((D_RNN, d)), _const_spec((1, d)), _const_spec((1, d))],
        out_specs=[pl.BlockSpec((1, t_rows, d), xmap), pl.BlockSpec((1, t_rows, d), xmap),
                   pl.BlockSpec((1, SUBLANES, D_RNN), smap), pl.BlockSpec((1, SUBLANES, D_RNN), smap)],
        out_shape=[jax.ShapeDtypeStruct((bsz, seq, d), F32), jax.ShapeDtypeStruct((bsz, seq, d), BF16),
                   jax.ShapeDtypeStruct((bsz, SUBLANES, D_RNN), F32),
                   jax.ShapeDtypeStruct((bsz, SUBLANES, D_RNN), F32)],
        scratch_shapes=[pltpu.VMEM((SUBLANES, D_RNN), F32), pltpu.VMEM((1, D_RNN), F32),
                        pltpu.VMEM((t_rows, D_RNN), F32), pltpu.VMEM((t_rows, D_RNN), F32)],
        compiler_params=_params("parallel", "arbitrary"),
        name="rglru_prompt",
    )(x_bf, x, w_in, conv_w, conv_b.reshape(1, D_RNN), wband, b_gate, lam.reshape(1, D_RNN), w_out,
      g.reshape(1, d), b.reshape(1, d))


def _rglru_sample_kernel(xbf_ref, x_ref, st_ref, h0_ref, win_ref, cw_ref, cb_ref, wband_ref, bg_ref, lam_ref,
                         wout_ref, g_ref, b_ref, y_ref, newst_ref, hlast_ref, h_sc, *, n_new):
    bd = h0_ref.shape[0]
    xb16 = xbf_ref[...]
    gate_br = _dot(xb16, win_ref[:, 0:D_RNN])
    xb = _dot(xb16, win_ref[:, D_RNN:2 * D_RNN])
    xpad = jnp.concatenate([st_ref[jj] for jj in range(CONV_W - 1)] + [xb], axis=0)
    xc = cb_ref[...]
    for jj in range(CONV_W):
        xc = xc + cw_ref[jj:jj + 1, :] * xpad[jj * bd:(jj + n_new) * bd]
    for jj in range(CONV_W - 1):
        newst_ref[jj] = xpad[(n_new + jj) * bd:(n_new + jj + 1) * bd]
    xc_bf = xc.astype(BF16)
    for j in range(N_GATE_TILES):
        lo = j * GATE_TILE
        w = min(GATE_TILE, D_RNN - lo)
        a, u = _rglru_gate_tile(j, xc, xc_bf, wband_ref, bg_ref, lam_ref)
        h = h0_ref[:, lo:lo + w]
        for t in range(n_new):
            h = a[t * bd:(t + 1) * bd] * h + u[t * bd:(t + 1) * bd]
            h_sc[t * bd:(t + 1) * bd, lo:lo + w] = h
        hlast_ref[:, lo:lo + w] = h
    mixed = (h_sc[...] * jax.nn.gelu(gate_br)).astype(BF16)
    y_ref[...] = _layer_norm(ALPHA * x_ref[...] + _dot(mixed, wout_ref[...]), g_ref[...], b_ref[...])


def rglru_sample(x_bf_tm, x_tm, state_tm, h0, w_in, conv_w, conv_b, wband, b_gate, lam, w_out, g, b):
    m, d = x_tm.shape
    bd = h0.shape[0]
    full = lambda shape: pl.BlockSpec(shape, lambda: (0,) * len(shape))
    args = (x_bf_tm, x_tm, state_tm, h0, w_in, conv_w, conv_b.reshape(1, D_RNN), wband, b_gate,
            lam.reshape(1, D_RNN), w_out, g.reshape(1, d), b.reshape(1, d))
    return pl.pallas_call(
        functools.partial(_rglru_sample_kernel, n_new=m // bd),
        in_specs=[full(a.shape) for a in args],
        out_specs=[full((m, d)), full((CONV_W - 1, bd, D_RNN)), full((bd, D_RNN))],
        out_shape=[jax.ShapeDtypeStruct((m, d), F32), jax.ShapeDtypeStruct((CONV_W - 1, bd, D_RNN), F32),
                   jax.ShapeDtypeStruct((bd, D_RNN), F32)],
        scratch_shapes=[pltpu.VMEM((m, D_RNN), F32)],
        compiler_params=pltpu.CompilerParams(vmem_limit_bytes=VMEM_LIMIT),
        name="rglru_sample",
    )(*args)


def _rope_table_kernel(cos_ref, sin_ref, *, pos0, period):
    n, half = cos_ref.shape
    r = lax.broadcasted_iota(jnp.int32, (n, half), 0)
    i = lax.broadcasted_iota(jnp.int32, (n, half), 1)
    pos = (pos0 + r % period).astype(F32)
    inv = jnp.exp(i.astype(F32) * (-math.log(ROPE_BASE) / half))
    ang = pos * inv
    cos_ref[...] = jnp.cos(ang)
    sin_ref[...] = jnp.sin(ang)


def rope_tables(n, pos0, period):
    half = DK_C // 2
    spec = pl.BlockSpec((n, half), lambda: (0, 0))
    return pl.pallas_call(
        functools.partial(_rope_table_kernel, pos0=pos0, period=period),
        out_specs=[spec, spec],
        out_shape=[jax.ShapeDtypeStruct((n, half), F32)] * 2,
        name="rope_tables",
    )()


def _ret_proj_kernel(x_ref, w_ref, cos_ref, sin_ref, q_ref, k_ref, v_ref, g_ref):
    x = x_ref[0]
    cos = cos_ref[...]
    sin = sin_ref[...]
    half = DK_C // 2
    qk_w = H_C * DK_C
    v_w = H_C * DV_C

    def roped(y, scale):
        outs = []
        for h in range(H_C):
            x1 = y[:, h * DK_C:h * DK_C + half]
            x2 = y[:, h * DK_C + half:(h + 1) * DK_C]
            outs += [x1 * cos - x2 * sin, x1 * sin + x2 * cos]
        return (jnp.concatenate(outs, axis=1) * scale).astype(BF16)

    q_ref[0] = roped(_dot(x, w_ref[:, 0:qk_w]), 1.0)
    k_ref[0] = roped(_dot(x, w_ref[:, qk_w:2 * qk_w]), DK_C ** -0.5)
    v_ref[0] = _dot(x, w_ref[:, 2 * qk_w:2 * qk_w + v_w]).astype(BF16)
    g_ref[0] = _dot(x, w_ref[:, 2 * qk_w + v_w:2 * qk_w + 2 * v_w])


def ret_proj(x_bf, w_in, cos, sin, tm):
    bsz, seq, d = x_bf.shape
    qk_w, v_w = H_C * DK_C, H_C * DV_C
    xmap = lambda b, i: (b, i, 0)
    tmap = lambda b, i: (i, 0)
    return pl.pallas_call(
        _ret_proj_kernel,
        grid=(bsz, seq // tm),
        in_specs=[pl.BlockSpec((1, tm, d), xmap), _const_spec((d, 2 * qk_w + 2 * v_w)),
                  pl.BlockSpec((tm, DK_C // 2), tmap), pl.BlockSpec((tm, DK_C // 2), tmap)],
        out_specs=[pl.BlockSpec((1, tm, qk_w), xmap), pl.BlockSpec((1, tm, qk_w), xmap),
                   pl.BlockSpec((1, tm, v_w), xmap), pl.BlockSpec((1, tm, v_w), xmap)],
        out_shape=[jax.ShapeDtypeStruct((bsz, seq, qk_w), BF16), jax.ShapeDtypeStruct((bsz, seq, qk_w), BF16),
                   jax.ShapeDtypeStruct((bsz, seq, v_w), BF16), jax.ShapeDtypeStruct((bsz, seq, v_w), F32)],
        compiler_params=_params("parallel", "parallel"),
        name="ret_proj",
    )(x_bf, w_in, cos, sin)


def _retention_kernel(lg_ref, q_ref, k_ref, v_ref, g_ref, gng_ref, gnb_ref, s0_ref, o_ref, sout_ref, s_sc,
                      *, t_true):
    c = pl.program_id(2)

    @pl.when(c == 0)
    def _load_state():
        s_sc[...] = s0_ref[...]

    lg = lg_ref[:, 0:1]
    q = q_ref[0]
    k = k_ref[0]
    v = v_ref[0]
    t = q.shape[0]
    ti = lax.broadcasted_iota(jnp.int32, (t, t), 0)
    si = lax.broadcasted_iota(jnp.int32, (t, t), 1)
    diff = (ti - si).astype(F32)
    decay = jnp.where(diff >= 0, jnp.exp(jnp.maximum(diff, 0.0) * lg), 0.0)
    scores = lax.dot_general(q, k, _NT, preferred_element_type=F32) * decay
    o = _dot(scores.astype(BF16), v)
    state = s_sc[...]
    tcol = lax.broadcasted_iota(jnp.int32, (t, 1), 0).astype(F32)
    o = o + _dot(q, state.astype(BF16)) * jnp.exp((tcol + 1.0) * lg)
    zeta = jnp.exp((t_true - 1.0 - tcol) * lg)
    kz = (k.astype(F32) * zeta).astype(BF16)
    new_state = jnp.exp(t_true * lg) * state + lax.dot_general(kz, v, _TN, preferred_element_type=F32)
    s_sc[...] = new_state

    mu = jnp.mean(o, axis=-1, keepdims=True)
    oc = o - mu
    var = jnp.mean(oc * oc, axis=-1, keepdims=True)
    on = oc * lax.rsqrt(var + GN_EPS) * gng_ref[...] + gnb_ref[...]
    o_ref[0] = (jax.nn.silu(g_ref[0]) * on).astype(BF16)

    @pl.when(c == pl.num_programs(2) - 1)
    def _store_state():
        sout_ref[...] = new_state


def retention(log_g, q, k, v, g, gn_g, gn_b, s0, chunk, t_true):
    bsz, seq, _ = q.shape
    v_w = H_C * DV_C
    qmap = lambda b, h, c: (b, c, h)
    smap = lambda b, h, c: (b, h, 0, 0)
    hmap = lambda b, h, c: (0, h)
    return pl.pallas_call(
        functools.partial(_retention_kernel, t_true=float(t_true)),
        grid=(bsz, H_C, seq // chunk),
        in_specs=[pl.BlockSpec((None, 1, LANES), lambda b, h, c: (h, 0, 0)),
                  pl.BlockSpec((1, chunk, DK_C), qmap), pl.BlockSpec((1, chunk, DK_C), qmap),
                  pl.BlockSpec((1, chunk, DV_C), qmap), pl.BlockSpec((1, chunk, DV_C), qmap),
                  pl.BlockSpec((1, DV_C), hmap), pl.BlockSpec((1, DV_C), hmap),
                  pl.BlockSpec((None, None, DK_C, DV_C), smap)],
        out_specs=[pl.BlockSpec((1, chunk, DV_C), qmap), pl.BlockSpec((None, None, DK_C, DV_C), smap)],
        out_shape=[jax.ShapeDtypeStruct((bsz, seq, v_w), BF16),
                   jax.ShapeDtypeStruct((bsz, H_C, DK_C, DV_C), F32)],
        scratch_shapes=[pltpu.VMEM((DK_C, DV_C), F32)],
        compiler_params=_params("parallel", "parallel", "arbitrary"),
        name="retention",
    )(log_g, q, k, v, g, gn_g.reshape(1, v_w), gn_b.reshape(1, v_w), s0)


TM_DENSE = 512
TQ_ATTN = 512
T_RGLRU = 512


def kernel(x_prompt, x_sample, cache_k_a, cache_v_a, cache_logf_a, page_table, state_conv_b, state_h_b, state_ret_c, w_in_a, b_f_a, w_out_a, w_in_b, conv_w_b, conv_b_b, w_gate_b, b_gate_b, lam_b, w_out_b, w_in_c, gn_g_c, gn_b_c, w_out_c, ln1_g, ln1_b, w_up_ffn, w_down_ffn, ln2_g, ln2_b):
    bsz, seq, d = x_prompt.shape
    bd, n_new, _ = x_sample.shape
    mp, ms = bsz * seq, bd * n_new
    past_len = page_table.shape[1] * cache_k_a.shape[2]

    xp = x_prompt.reshape(mp, d)
    xs = x_sample.reshape(ms, d)
    xp_bf = xp.astype(BF16)
    xs_bf = xs.astype(BF16)

    n_la, pool, page = cache_k_a.shape[:3]
    kT_pool = jnp.transpose(cache_k_a, (0, 1, 3, 4, 2)).reshape(n_la, pool, d, page)
    vT_pool = jnp.transpose(cache_v_a, (0, 1, 3, 4, 2)).reshape(n_la, pool, d, page)
    lfT_pool = jnp.transpose(cache_logf_a, (0, 1, 3, 2))

    log_g = jnp.log1p(-jnp.exp2(-5.0 - jnp.arange(H_C, dtype=F32)))
    log_g = jnp.broadcast_to(log_g[:, None, None], (H_C, 1, LANES))

    outs_a = {name: [] for name in ("kp", "vp", "lfp", "ks", "vs", "lfs")}
    out_b, out_c = {}, {}

    for i in range(DEPTH):
        jl, kind = divmod(i, 3)
        if kind == 0:
            w = w_in_a[jl]
            wq = w[:, 0:d].astype(BF16)
            wkT = w[:, d:2 * d].T.astype(BF16)
            wvT = w[:, 2 * d:3 * d].T.astype(BF16)
            wfT = w[:, 3 * d:].T.astype(BF16)
            w_out = w_out_a[jl].astype(BF16)
            q_bf, kT, vT, kT_bf, vT_bf, lfT = fox_proj(
                xp_bf.reshape(bsz, seq, d), wq, wkT, wvT, wfT, b_f_a[jl].reshape(H_A, 1), TM_DENSE)
            ct = fox_cumsum(lfT)
            o_bf = fox_attn(q_bf, kT_bf, vT_bf, jnp.transpose(ct, (0, 2, 1)), ct, TQ_ATTN)
            mix_p = (o_bf.reshape(mp, d), w_out)
            outs_a["kp"].append(kT)
            outs_a["vp"].append(vT)
            outs_a["lfp"].append(lfT)
            wf_pad = jnp.pad(w[:, 3 * d:], ((0, 0), (0, LANES - H_A))).astype(BF16)
            bf_pad = jnp.pad(b_f_a[jl], (0, LANES - H_A)).reshape(1, LANES)
            qs, ks, vs, lfs = fox_proj_sample(xs_bf, w[:, 0:3 * d].astype(BF16), wf_pad, bf_pad)
            lfs = lfs[:, :H_A].reshape(bd, n_new, H_A)
            lfs_T = jnp.pad(jnp.transpose(lfs, (0, 2, 1)), ((0, 0), (0, 0), (0, LANES - n_new)))
            o_s = fox_sample_attn(page_table, kT_pool, vT_pool, lfT_pool, jl, qs.reshape(bd, n_new, d),
                                  ks.reshape(bd, n_new, d), vs.reshape(bd, n_new, d), lfs_T)
            mix_s = (o_s.reshape(ms, d).astype(BF16), w_out)
            outs_a["ks"].append(ks.reshape(bd, n_new, H_A, DH_A))
            outs_a["vs"].append(vs.reshape(bd, n_new, H_A, DH_A))
            outs_a["lfs"].append(lfs)
        elif kind == 1:
            w_in = w_in_b[jl].astype(BF16)
            w_out = w_out_b[jl].astype(BF16)
            wband = _band_gate_weights(w_gate_b[jl])
            xp3, xp_bf3, tail, hlast = rglru_prompt(
                xp_bf.reshape(bsz, seq, d), xp.reshape(bsz, seq, d), w_in, conv_w_b[jl], conv_b_b[jl], wband,
                b_gate_b[jl], lam_b[jl], w_out, ln1_g[i], ln1_b[i], T_RGLRU)
            xp, xp_bf = xp3.reshape(mp, d), xp_bf3.reshape(mp, d)
            out_b["cp"] = tail[:, SUBLANES - (CONV_W - 1):, :]
            out_b["hp"] = hlast[:, 0, :]
            to_tm = lambda a: jnp.transpose(a.reshape(bd, n_new, d), (1, 0, 2)).reshape(ms, d)
            ys_tm, newst, hs = rglru_sample(
                to_tm(xs_bf), to_tm(xs), jnp.transpose(state_conv_b[jl], (1, 0, 2)), state_h_b[jl], w_in,
                conv_w_b[jl], conv_b_b[jl], wband, b_gate_b[jl], lam_b[jl], w_out, ln1_g[i], ln1_b[i])
            xs = jnp.transpose(ys_tm.reshape(n_new, bd, d), (1, 0, 2)).reshape(ms, d)
            xs_bf = xs.astype(BF16)
            out_b["cs"] = jnp.transpose(newst, (1, 0, 2))
            out_b["hs"] = hs
            mix_p = mix_s = None
        else:
            w_in = w_in_c[jl].astype(BF16)
            w_out = w_out_c[jl].astype(BF16)
            v_w = H_C * DV_C
            cos_p, sin_p = rope_tables(seq, 0, seq)
            q, k, v, g = ret_proj(xp_bf.reshape(bsz, seq, d), w_in, cos_p, sin_p, TM_DENSE)
            chunk = CHUNK_C if seq % CHUNK_C == 0 else seq
            og, s_p = retention(log_g, q, k, v, g, gn_g_c[jl], gn_b_c[jl],
                                jnp.zeros((bsz, H_C, DK_C, DV_C), F32), chunk, chunk)
            mix_p = (og.reshape(mp, v_w), w_out)
            out_c["sp"] = s_p
            cos_s, sin_s = rope_tables(ms, past_len, n_new)
            q, k, v, g = ret_proj(xs_bf.reshape(1, ms, d), w_in, cos_s, sin_s, ms)
            pad_rows = lambda a: jnp.pad(a.reshape(bd, n_new, a.shape[-1]), ((0, 0), (0, SUBLANES - n_new), (0, 0)))
            og, s_s = retention(log_g, pad_rows(q), pad_rows(k), pad_rows(v), pad_rows(g), gn_g_c[jl], gn_b_c[jl],
                                state_ret_c[jl], SUBLANES, n_new)
            mix_s = (og[:, :n_new].reshape(ms, v_w), w_out)
            out_c["ss"] = s_s
        if mix_p is not None:
            xp, xp_bf = proj_ln(mix_p[0], mix_p[1], xp, ln1_g[i], ln1_b[i], TM_DENSE)
            xs, xs_bf = proj_ln(mix_s[0], mix_s[1], xs, ln1_g[i], ln1_b[i], TM_DENSE)
        w_up = w_up_ffn[i].astype(BF16)
        w_dn = w_down_ffn[i].astype(BF16)
        xp, xp_bf = ffn_ln(xp_bf, xp, w_up, w_dn, ln2_g[i], ln2_b[i], TM_DENSE)
        xs, xs_bf = ffn_ln(xs_bf, xs, w_up, w_dn, ln2_g[i], ln2_b[i], TM_DENSE)

    kv_out = lambda lst: jnp.transpose(jnp.stack(lst).reshape(n_la, bsz, H_A, DH_A, seq), (0, 1, 4, 2, 3))
    return (xp.reshape(bsz, seq, d), xs.reshape(bd, n_new, d),
            kv_out(outs_a["kp"]), kv_out(outs_a["vp"]), jnp.transpose(jnp.stack(outs_a["lfp"]), (0, 1, 3, 2)),
            jnp.stack(outs_a["ks"]), jnp.stack(outs_a["vs"]), jnp.stack(outs_a["lfs"]),
            out_b["cp"][None], out_b["hp"][None], out_b["cs"][None], out_b["hs"][None],
            out_c["sp"][None], out_c["ss"][None])
```

```python
import functools
import math

import jax
import jax.numpy as jnp
from jax import lax
from jax.experimental import pallas as pl
from jax.experimental.pallas import tpu as pltpu

F32 = jnp.float32
BF16 = jnp.bfloat16

D_MODEL = 1024
DEPTH = 4
H_A = 16
DH_A = D_MODEL // H_A
D_RNN = 1408
N_BLK_B = 16
BW_B = D_RNN // N_BLK_B
CONV_W = 4
LRU_C = 8.0
H_C = 4
DK_C = D_MODEL // H_C
DV_C = 2 * DK_C
CHUNK_C = 128
ROPE_BASE = 10000.0
D_FF = 2816
ALPHA = (2 * DEPTH) ** 0.25
LN_EPS = 1e-5
GN_EPS = 1e-5

LANES = 128
SUBLANES = 8
MXU_COLS = 256
VMEM_LIMIT = 56 * 1024 * 1024

_NT = (((1,), (1,)), ((), ()))
_TN = (((0,), (0,)), ((), ()))


def _params(*sem):
    return pltpu.CompilerParams(dimension_semantics=sem, vmem_limit_bytes=VMEM_LIMIT)


def _const_spec(shape):
    zeros = (0,) * len(shape)
    return pl.BlockSpec(shape, lambda *_: zeros, pipeline_mode=pl.Buffered(1))


def _dot(a, b):
    return jnp.dot(a, b, preferred_element_type=F32)


def _layer_norm(z, g, b):
    mu = jnp.mean(z, axis=-1, keepdims=True)
    zc = z - mu
    var = jnp.mean(zc * zc, axis=-1, keepdims=True)
    return zc * lax.rsqrt(var + LN_EPS) * g + b


def _log_sigmoid(z):
    return jnp.minimum(z, 0.0) - jnp.log1p(jnp.exp(-jnp.abs(z)))


def _dot_f32_by_01(x, m01):
    hi = x.astype(BF16)
    r1 = x - hi.astype(F32)
    mid = r1.astype(BF16)
    lo = (r1 - mid.astype(F32)).astype(BF16)
    return _dot(hi, m01) + _dot(mid, m01) + _dot(lo, m01)


def _tri01(n, strict_lower_rows):
    r = lax.broadcasted_iota(jnp.int32, (n, n), 0)
    c = lax.broadcasted_iota(jnp.int32, (n, n), 1)
    keep = (r > c) if strict_lower_rows else (r <= c)
    return jnp.where(keep, 1.0, 0.0).astype(BF16)


def _proj_ln_kernel(a_ref, w_ref, x_ref, g_ref, b_ref, y_ref, ybf_ref):
    m = _dot(a_ref[...], w_ref[...])
    y = _layer_norm(ALPHA * x_ref[...] + m, g_ref[...], b_ref[...])
    y_ref[...] = y
    ybf_ref[...] = y.astype(BF16)


def proj_ln(a_bf, w_bf, x, g, b, tm):
    m, k = a_bf.shape
    d = w_bf.shape[1]
    tm = min(tm, m)
    row = lambda i: (i, 0)
    return pl.pallas_call(
        _proj_ln_kernel,
        grid=(m // tm,),
        in_specs=[pl.BlockSpec((tm, k), row), _const_spec((k, d)), pl.BlockSpec((tm, d), row),
                  _const_spec((1, d)), _const_spec((1, d))],
        out_specs=[pl.BlockSpec((tm, d), row), pl.BlockSpec((tm, d), row)],
        out_shape=[jax.ShapeDtypeStruct((m, d), F32), jax.ShapeDtypeStruct((m, d), BF16)],
        compiler_params=_params("parallel"),
        name="proj_ln",
    )(a_bf, w_bf, x, g.reshape(1, d), b.reshape(1, d))


def _ffn_ln_kernel(xbf_ref, x_ref, wup_ref, wdn_ref, g_ref, b_ref, y_ref, ybf_ref, *, n_chunks):
    xb = xbf_ref[...]
    cw = D_FF // n_chunks
    acc = None
    for c in range(n_chunks):
        gate = _dot(xb, wup_ref[:, c * cw:(c + 1) * cw])
        up = _dot(xb, wup_ref[:, D_FF + c * cw:D_FF + (c + 1) * cw])
        h = (jax.nn.silu(gate) * up).astype(BF16)
        part = _dot(h, wdn_ref[c * cw:(c + 1) * cw, :])
        acc = part if acc is None else acc + part
    y = _layer_norm(ALPHA * x_ref[...] + acc, g_ref[...], b_ref[...])
    y_ref[...] = y
    ybf_ref[...] = y.astype(BF16)


def ffn_ln(x_bf, x, w_up_bf, w_dn_bf, g, b, tm, n_chunks=2):
    m, d = x.shape
    tm = min(tm, m)
    row = lambda i: (i, 0)
    return pl.pallas_call(
        functools.partial(_ffn_ln_kernel, n_chunks=n_chunks),
        grid=(m // tm,),
        in_specs=[pl.BlockSpec((tm, d), row), pl.BlockSpec((tm, d), row),
                  _const_spec((d, 2 * D_FF)), _const_spec((D_FF, d)),
                  _const_spec((1, d)), _const_spec((1, d))],
        out_specs=[pl.BlockSpec((tm, d), row), pl.BlockSpec((tm, d), row)],
        out_shape=[jax.ShapeDtypeStruct((m, d), F32), jax.ShapeDtypeStruct((m, d), BF16)],
        compiler_params=_params("parallel"),
        name="ffn_ln",
    )(x_bf, x, w_up_bf, w_dn_bf, g.reshape(1, d), b.reshape(1, d))


def _fox_proj_kernel(x_ref, wq_ref, wkT_ref, wvT_ref, wfT_ref, bf_ref,
                     q_ref, kT_ref, vT_ref, kTb_ref, vTb_ref, lfT_ref):
    x = x_ref[0]
    q_ref[0] = (_dot(x, wq_ref[...]) * DH_A ** -0.5).astype(BF16)
    kT = lax.dot_general(wkT_ref[...], x, _NT, preferred_element_type=F32)
    kT_ref[0] = kT
    kTb_ref[0] = kT.astype(BF16)
    vT = lax.dot_general(wvT_ref[...], x, _NT, preferred_element_type=F32)
    vT_ref[0] = vT
    vTb_ref[0] = vT.astype(BF16)
    f = lax.dot_general(wfT_ref[...], x, _NT, preferred_element_type=F32)
    lfT_ref[0] = _log_sigmoid(f + bf_ref[...])


def fox_proj(x_bf, wq, wkT, wvT, wfT, bf_col, tm):
    bsz, seq, d = x_bf.shape
    xmap = lambda b, i: (b, i, 0)
    tmap = lambda b, i: (b, 0, i)
    return pl.pallas_call(
        _fox_proj_kernel,
        grid=(bsz, seq // tm),
        in_specs=[pl.BlockSpec((1, tm, d), xmap), _const_spec((d, d)), _const_spec((d, d)),
                  _const_spec((d, d)), _const_spec((H_A, d)), _const_spec((H_A, 1))],
        out_specs=[pl.BlockSpec((1, tm, d), xmap), pl.BlockSpec((1, d, tm), tmap),
                   pl.BlockSpec((1, d, tm), tmap), pl.BlockSpec((1, d, tm), tmap),
                   pl.BlockSpec((1, d, tm), tmap), pl.BlockSpec((1, H_A, tm), tmap)],
        out_shape=[jax.ShapeDtypeStruct((bsz, seq, d), BF16),
                   jax.ShapeDtypeStruct((bsz, d, seq), F32), jax.ShapeDtypeStruct((bsz, d, seq), F32),
                   jax.ShapeDtypeStruct((bsz, d, seq), BF16), jax.ShapeDtypeStruct((bsz, d, seq), BF16),
                   jax.ShapeDtypeStruct((bsz, H_A, seq), F32)],
        compiler_params=_params("parallel", "parallel"),
        name="fox_proj",
    )(x_bf, wq, wkT, wvT, wfT, bf_col)


def _fox_cumsum_kernel(lf_ref, ct_ref, *, chunk):
    seq = lf_ref.shape[2]
    prefix = _tri01(chunk, strict_lower_rows=False)
    carry = jnp.zeros((H_A, 1), F32)
    for i in range(seq // chunk):
        cs = _dot_f32_by_01(lf_ref[0, :, i * chunk:(i + 1) * chunk], prefix) + carry
        ct_ref[0, :, i * chunk:(i + 1) * chunk] = cs
        carry = cs[:, chunk - 1:chunk]


def fox_cumsum(lfT):
    bsz, h, seq = lfT.shape
    chunk = min(seq, MXU_COLS)
    spec = pl.BlockSpec((1, h, seq), lambda b: (b, 0, 0))
    return pl.pallas_call(
        functools.partial(_fox_cumsum_kernel, chunk=chunk),
        grid=(bsz,), in_specs=[spec], out_specs=spec,
        out_shape=jax.ShapeDtypeStruct((bsz, h, seq), F32),
        compiler_params=_params("parallel"),
        name="fox_cumsum",
    )(lfT)


def _fox_attn_kernel(qi_ref, kj_ref, q_ref, kT_ref, vT_ref, c_ref, ct_ref, o_ref, cq_sc, m_sc, l_sc, acc_sc, *, tq):
    p = pl.program_id(1)
    i = qi_ref[pl.program_id(2)]
    j = kj_ref[pl.program_id(2)]
    tk = kT_ref.shape[2]
    lane = lax.broadcasted_iota(jnp.int32, (tq, LANES), 1)
    first_head = lane < DH_A

    def spread(x):
        return jnp.concatenate([x] * (tk // LANES), axis=1)

    @pl.when(j == 0)
    def _init():
        cblk = c_ref[0]
        hl = lax.broadcasted_iota(jnp.int32, cblk.shape, 1)
        for a in range(2):
            cq = jnp.sum(jnp.where(hl == 2 * p + a, cblk, 0.0), axis=1, keepdims=True)
            cq_sc[a] = jnp.broadcast_to(cq, (tq, LANES))
        m_sc[...] = jnp.full(m_sc.shape, -jnp.inf, F32)
        l_sc[...] = jnp.zeros(l_sc.shape, F32)
        acc_sc[...] = jnp.zeros(acc_sc.shape, F32)

    def step(diagonal):
        q = q_ref[0]
        kT = kT_ref[0]
        vT = vT_ref[0]
        alphas, pvs = [], []
        for a in range(2):
            qa = jnp.where(first_head if a == 0 else jnp.logical_not(first_head), q, jnp.zeros_like(q))
            ck = ct_ref[0, pl.ds(2 * p + a, 1), :]
            s = _dot(qa, kT) + (spread(cq_sc[a]) - ck)
            if diagonal:
                r = lax.broadcasted_iota(jnp.int32, s.shape, 0)
                c = lax.broadcasted_iota(jnp.int32, s.shape, 1)
                s = jnp.where(r >= c, s, -jnp.inf)
            m_old = m_sc[a]
            m_new = jnp.maximum(m_old, jnp.max(s, axis=1, keepdims=True))
            alpha = jnp.exp(m_old - m_new)
            pr = jnp.exp(s - spread(m_new))
            l_sc[a] = alpha * l_sc[a] + jnp.sum(pr, axis=1, keepdims=True)
            m_sc[a] = m_new
            pvs.append(lax.dot_general(pr.astype(BF16), vT, _NT, preferred_element_type=F32))
            alphas.append(alpha)
        acc_sc[...] = (acc_sc[...] * jnp.where(first_head, alphas[0], alphas[1])
                       + jnp.where(first_head, pvs[0], pvs[1]))

    @pl.when(j < i)
    def _below():
        step(False)

    @pl.when(j == i)
    def _diag():
        step(True)
        o_ref[0] = (acc_sc[...] / jnp.where(first_head, l_sc[0], l_sc[1])).astype(BF16)


def fox_attn(q_bf, kT_bf, vT_bf, c, ct, tq):
    bsz, seq, d = q_bf.shape
    nq = seq // tq
    pairs = [(i, j) for i in range(nq) for j in range(i + 1)]
    qi = jnp.asarray([ij[0] for ij in pairs], jnp.int32)
    kj = jnp.asarray([ij[1] for ij in pairs], jnp.int32)
    qmap = lambda b, p, s, qi, kj: (b, qi[s], p)
    kmap = lambda b, p, s, qi, kj: (b, p, kj[s])
    grid_spec = pltpu.PrefetchScalarGridSpec(
        num_scalar_prefetch=2,
        grid=(bsz, H_A // 2, len(pairs)),
        in_specs=[pl.BlockSpec((1, tq, LANES), qmap), pl.BlockSpec((1, LANES, tq), kmap),
                  pl.BlockSpec((1, LANES, tq), kmap),
                  pl.BlockSpec((1, tq, H_A), lambda b, p, s, qi, kj: (b, qi[s], 0)),
                  pl.BlockSpec((1, H_A, tq), lambda b, p, s, qi, kj: (b, 0, kj[s]))],
        out_specs=pl.BlockSpec((1, tq, LANES), qmap),
        scratch_shapes=[pltpu.VMEM((2, tq, LANES), F32), pltpu.VMEM((2, tq, LANES), F32),
                        pltpu.VMEM((2, tq, LANES), F32), pltpu.VMEM((tq, LANES), F32)],
    )
    return pl.pallas_call(
        functools.partial(_fox_attn_kernel, tq=tq),
        grid_spec=grid_spec,
        out_shape=jax.ShapeDtypeStruct((bsz, seq, d), BF16),
        compiler_params=_params("parallel", "parallel", "arbitrary"),
        name="fox_attn",
    )(qi, kj, q_bf, kT_bf, vT_bf, c, ct)


def _fox_proj_sample_kernel(x_ref, w_ref, wf_ref, bf_ref, q_ref, k_ref, v_ref, lf_ref):
    x = x_ref[...]
    d = x.shape[1]
    q_ref[...] = _dot(x, w_ref[:, 0:d])
    k_ref[...] = _dot(x, w_ref[:, d:2 * d])
    v_ref[...] = _dot(x, w_ref[:, 2 * d:3 * d])
    lf_ref[...] = _log_sigmoid(_dot(x, wf_ref[...]) + bf_ref[...])


def fox_proj_sample(x_bf, w_qkv, wf_pad, bf_pad):
    m, d = x_bf.shape
    full = lambda shape: pl.BlockSpec(shape, lambda: (0,) * len(shape))
    return pl.pallas_call(
        _fox_proj_sample_kernel,
        in_specs=[full((m, d)), full((d, 3 * d)), full((d, LANES)), full((1, LANES))],
        out_specs=[full((m, d)), full((m, d)), full((m, d)), full((m, LANES))],
        out_shape=[jax.ShapeDtypeStruct((m, d), F32)] * 3 + [jax.ShapeDtypeStruct((m, LANES), F32)],
        compiler_params=pltpu.CompilerParams(vmem_limit_bytes=VMEM_LIMIT),
        name="fox_proj_sample",
    )(x_bf, w_qkv, wf_pad, bf_pad)


def _fox_sample_attn_kernel(pt_ref, *refs, n_new, group):
    kT_refs, vT_refs, lf_refs = refs[0:group], refs[group:2 * group], refs[2 * group:3 * group]
    (q_ref, kn_ref, vn_ref, lfs_ref, o_ref,
     qbd_sc, ctn_sc, ctcol_sc, m_sc, l_sc, acc_sc, carry_sc) = refs[3 * group:]
    j = pl.program_id(1)
    rows = n_new * H_A
    d = qbd_sc.shape[1]
    page = kT_refs[0].shape[1]

    def head_diag():
        r = lax.broadcasted_iota(jnp.int32, (rows, d), 0)
        c = lax.broadcasted_iota(jnp.int32, (rows, d), 1)
        return (c // DH_A) == (r % H_A)

    def spread(x, width):
        return jnp.concatenate([x] * (width // LANES), axis=1) if width >= LANES else x[:, 0:width]

    def online_update(s, pv_fn):
        m_old = m_sc[...]
        m_new = jnp.maximum(m_old, jnp.max(s, axis=1, keepdims=True))
        alpha = jnp.exp(m_old - m_new)
        pr = jnp.exp(s - spread(m_new, s.shape[1]))
        l_sc[...] = alpha * l_sc[...] + jnp.sum(pr, axis=1, keepdims=True)
        m_sc[...] = m_new
        acc_sc[...] = spread(alpha, d) * acc_sc[...] + pv_fn(pr.astype(BF16))

    @pl.when(j == 0)
    def _init():
        q = q_ref[0]
        qe = jnp.concatenate([jnp.broadcast_to(q[t:t + 1, :], (H_A, d)) for t in range(n_new)], axis=0)
        qbd_sc[...] = jnp.where(head_diag(), qe * DH_A ** -0.5, 0.0).astype(BF16)
        ctn = _dot_f32_by_01(lfs_ref[0], _tri01(LANES, strict_lower_rows=False))
        ctn_sc[...] = ctn
        ctcol = jnp.concatenate([ctn[:, t:t + 1] for t in range(n_new)], axis=0)
        ctcol_sc[...] = jnp.broadcast_to(ctcol, ctcol_sc.shape)
        m_sc[...] = jnp.full(m_sc.shape, -jnp.inf, F32)
        l_sc[...] = jnp.zeros(l_sc.shape, F32)
        acc_sc[...] = jnp.zeros(acc_sc.shape, F32)
        carry_sc[...] = jnp.zeros(carry_sc.shape, F32)

    qbd = qbd_sc[...]
    lf_all = jnp.concatenate([lf_refs[g][...] for g in range(group)], axis=0)
    suffix_all = _dot_f32_by_01(lf_all, _tri01(page, strict_lower_rows=True))
    carry = carry_sc[...]
    parts = []
    for g in range(group):
        bias = suffix_all[g * H_A:(g + 1) * H_A] + carry
        carry = carry + jnp.sum(lf_refs[g][...], axis=1, keepdims=True)
        parts.append(_dot(qbd, kT_refs[g][...].astype(BF16)) + jnp.concatenate([bias] * n_new, axis=0))
    carry_sc[...] = carry
    s = jnp.concatenate(parts, axis=1) + spread(ctcol_sc[...], group * page)

    def pv_pages(p):
        out = None
        for g in range(group):
            t = lax.dot_general(p[:, g * page:(g + 1) * page], vT_refs[g][...].astype(BF16), _NT,
                                preferred_element_type=F32)
            out = t if out is None else out + t
        return out

    online_update(s, pv_pages)

    @pl.when(j == pl.num_programs(1) - 1)
    def _finish():
        pad = jnp.zeros((SUBLANES - n_new, d), F32)
        kn = jnp.concatenate([kn_ref[0], pad], axis=0).astype(BF16)
        vn = jnp.concatenate([vn_ref[0], pad], axis=0).astype(BF16)
        sn = lax.dot_general(qbd_sc[...], kn, _NT, preferred_element_type=F32)
        ctk = jnp.concatenate([ctn_sc[:, 0:SUBLANES]] * n_new, axis=0)
        tq = lax.broadcasted_iota(jnp.int32, sn.shape, 0) // H_A
        tk = lax.broadcasted_iota(jnp.int32, sn.shape, 1)
        sn = jnp.where(tk <= tq, sn + (ctcol_sc[:, 0:SUBLANES] - ctk), -jnp.inf)
        online_update(sn, lambda p: _dot(p, vn))
        om = jnp.where(head_diag(), acc_sc[...] / spread(l_sc[...], d), 0.0)
        o_ref[0] = jnp.concatenate(
            [jnp.sum(om[t * H_A:(t + 1) * H_A], axis=0, keepdims=True) for t in range(n_new)], axis=0)


def fox_sample_attn(page_table, kT_pool, vT_pool, lfT_pool, layer, q, k_new, v_new, lfs_T, group):
    bd, n_new, d = q.shape
    n_pages = page_table.shape[1]
    page = kT_pool.shape[3]
    rows = n_new * H_A
    group = math.gcd(group, n_pages)
    pmaps = [lambda b, j, pt, g=g: (layer, pt[b, n_pages - 1 - (j * group + g)], 0, 0) for g in range(group)]
    bmap = lambda b, j, pt: (b, 0, 0)
    grid_spec = pltpu.PrefetchScalarGridSpec(
        num_scalar_prefetch=1,
        grid=(bd, n_pages // group),
        in_specs=([pl.BlockSpec((None, None, d, page), pm) for pm in pmaps]
                  + [pl.BlockSpec((None, None, d, page), pm) for pm in pmaps]
                  + [pl.BlockSpec((None, None, H_A, page), pm) for pm in pmaps]
                  + [pl.BlockSpec((1, n_new, d), bmap), pl.BlockSpec((1, n_new, d), bmap),
                     pl.BlockSpec((1, n_new, d), bmap), pl.BlockSpec((1, H_A, LANES), bmap)]),
        out_specs=pl.BlockSpec((1, n_new, d), bmap),
        scratch_shapes=[pltpu.VMEM((rows, d), BF16), pltpu.VMEM((H_A, LANES), F32),
                        pltpu.VMEM((rows, LANES), F32), pltpu.VMEM((rows, LANES), F32),
                        pltpu.VMEM((rows, LANES), F32), pltpu.VMEM((rows, d), F32), pltpu.VMEM((H_A, LANES), F32)],
    )
    return pl.pallas_call(
        functools.partial(_fox_sample_attn_kernel, n_new=n_new, group=group),
        grid_spec=grid_spec,
        out_shape=jax.ShapeDtypeStruct((bd, n_new, d), F32),
        compiler_params=_params("parallel", "arbitrary"),
        name="fox_sample_attn",
    )(page_table, *([kT_pool] * group), *([vT_pool] * group), *([lfT_pool] * group), q, k_new, v_new, lfs_T)


GATE_TILE = MXU_COLS
GATE_WIN = 2 * MXU_COLS
N_GATE_TILES = -(-D_RNN // GATE_TILE)


def _gate_window_start(j):
    first_block = (j * GATE_TILE) // BW_B
    return min((first_block * BW_B) // LANES * LANES, D_RNN - GATE_WIN)


def _band_gate_weights(w_gate):
    eye = jnp.eye(N_BLK_B, dtype=w_gate.dtype)
    tiles = []
    for part in range(2):
        wp = w_gate[:, :, part * BW_B:(part + 1) * BW_B]
        dense = (eye[:, None, :, None] * wp[:, :, None, :]).reshape(D_RNN, D_RNN)
        dense = jnp.pad(dense, ((0, 0), (0, N_GATE_TILES * GATE_TILE - D_RNN)))
        tiles.append(jnp.stack([
            dense[_gate_window_start(j):_gate_window_start(j) + GATE_WIN, j * GATE_TILE:(j + 1) * GATE_TILE]
            for j in range(N_GATE_TILES)]))
    return jnp.stack(tiles).astype(BF16)


def _rglru_gate_tile(j, xc, xc_bf, wband_ref, bg_ref, lam_ref):
    lo = j * GATE_TILE
    w = min(GATE_TILE, D_RNN - lo)
    ks = _gate_window_start(j)
    xw = xc_bf[:, ks:ks + GATE_WIN]
    r = jax.nn.sigmoid(_dot(xw, wband_ref[0, j])[:, :w] + bg_ref[0:1, lo:lo + w])
    ig = jax.nn.sigmoid(_dot(xw, wband_ref[1, j])[:, :w] + bg_ref[1:2, lo:lo + w])
    log_a = LRU_C * r * _log_sigmoid(lam_ref[:, lo:lo + w])
    a = jnp.exp(log_a)
    u = jnp.sqrt(-jnp.tanh(log_a) * (a * a + 1.0)) * (ig * xc[:, lo:lo + w])
    return a, u


def _rglru_prompt_kernel(xbf_ref, x_ref, win_ref, cw_ref, cb_ref, wband_ref, bg_ref, lam_ref, wout_ref,
                         g_ref, b_ref, y_ref, ybf_ref, tail_ref, hlast_ref, tail_sc, h_sc, a_sc, u_sc):
    t_rows = xbf_ref.shape[1]

    @pl.when(pl.program_id(1) == 0)
    def _reset():
        tail_sc[...] = jnp.zeros(tail_sc.shape, F32)
        h_sc[...] = jnp.zeros(h_sc.shape, F32)

    xb16 = xbf_ref[0]
    gate_br = _dot(xb16, win_ref[:, 0:D_RNN])
    xb = _dot(xb16, win_ref[:, D_RNN:2 * D_RNN])

    row8 = lax.broadcasted_iota(jnp.int32, (SUBLANES, D_RNN), 0)
    tail = tail_sc[...]
    xc = cb_ref[...] + cw_ref[CONV_W - 1:CONV_W, :] * xb
    for k in range(1, CONV_W):
        xs = pltpu.roll(xb, k, axis=0)
        head = jnp.where(row8 < k, pltpu.roll(tail, k, axis=0), xs[0:SUBLANES])
        xs = jnp.concatenate([head, xs[SUBLANES:]], axis=0)
        xc = xc + cw_ref[CONV_W - 1 - k:CONV_W - k, :] * xs
    new_tail = xb[t_rows - SUBLANES:t_rows]
    tail_sc[...] = new_tail
    tail_ref[0] = new_tail
    xc_bf = xc.astype(BF16)

    for j in range(N_GATE_TILES):
        lo = j * GATE_TILE
        w = min(GATE_TILE, D_RNN - lo)
        a, u = _rglru_gate_tile(j, xc, xc_bf, wband_ref, bg_ref, lam_ref)
        pos = lax.broadcasted_iota(jnp.int32, a.shape, 0) % SUBLANES
        for s in (1, 2, 4):
            live = pos >= s
            u = jnp.where(live, a * pltpu.roll(u, s, axis=0) + u, u)
            a = jnp.where(live, a * pltpu.roll(a, s, axis=0), a)
        a_sc[:, lo:lo + w] = a
        u_sc[:, lo:lo + w] = u

        def group(i, h_prev, lo=lo, w=w):
            blk = pl.ds(pl.multiple_of(i * SUBLANES, SUBLANES), SUBLANES)
            hb = a_sc[blk, lo:lo + w] * h_prev + u_sc[blk, lo:lo + w]
            u_sc[blk, lo:lo + w] = hb
            return hb[SUBLANES - 1:SUBLANES, :]

        h_sc[:, lo:lo + w] = lax.fori_loop(0, t_rows // SUBLANES, group, h_sc[:, lo:lo + w])

    hlast_ref[0] = jnp.broadcast_to(h_sc[...], (SUBLANES, D_RNN))
    mixed = (u_sc[...] * jax.nn.gelu(gate_br)).astype(BF16)
    y = _layer_norm(ALPHA * x_ref[0] + _dot(mixed, wout_ref[...]), g_ref[...], b_ref[...])
    y_ref[0] = y
    ybf_ref[0] = y.astype(BF16)


def rglru_prompt(x_bf, x, w_in, conv_w, conv_b, wband, b_gate, lam, w_out, g, b, t_rows):
    bsz, seq, d = x.shape
    xmap = lambda bi, ti: (bi, ti, 0)
    smap = lambda bi, ti: (bi, 0, 0)
    return pl.pallas_call(
        _rglru_prompt_kernel,
        grid=(bsz, seq // t_rows),
        in_specs=[pl.BlockSpec((1, t_rows, d), xmap), pl.BlockSpec((1, t_rows, d), xmap),
                  _const_spec((d, 2 * D_RNN)), _const_spec((CONV_W, D_RNN)), _const_spec((1, D_RNN)),
                  _const_spec((2, N_GATE_TILES, GATE_WIN, GATE_TILE)), _const_spec((2, D_RNN)),
                  _const_spec((1, D_RNN)), _const_spec((D_RNN, d)), _const_spec((1, d)), _const_spec((1, d))],
        out_specs=[pl.BlockSpec((1, t_rows, d), xmap), pl.BlockSpec((1, t_rows, d), xmap),
                   pl.BlockSpec((1, SUBLANES, D_RNN), smap), pl.BlockSpec((1, SUBLANES, D_RNN), smap)],
        out_shape=[jax.ShapeDtypeStruct((bsz, seq, d), F32), jax.ShapeDtypeStruct((bsz, seq, d), BF16),
                   jax.ShapeDtypeStruct((bsz, SUBLANES, D_RNN), F32),
                   jax.ShapeDtypeStruct((bsz, SUBLANES, D_RNN), F32)],
        scratch_shapes=[pltpu.VMEM((SUBLANES, D_RNN), F32), pltpu.VMEM((1, D_RNN), F32),
                        pltpu.VMEM((t_rows, D_RNN), F32), pltpu.VMEM((t_rows, D_RNN), F32)],
        compiler_params=_params("parallel", "arbitrary"),
        name="rglru_prompt",
    )(x_bf, x, w_in, conv_w, conv_b.reshape(1, D_RNN), wband, b_gate, lam.reshape(1, D_RNN), w_out,
      g.reshape(1, d), b.reshape(1, d))


def _rglru_sample_kernel(xbf_ref, x_ref, st_ref, h0_ref, win_ref, cw_ref, cb_ref, wband_ref, bg_ref, lam_ref,
                         wout_ref, g_ref, b_ref, y_ref, newst_ref, hlast_ref, h_sc, *, n_new):
    bd = h0_ref.shape[0]
    xb16 = xbf_ref[...]
    gate_br = _dot(xb16, win_ref[:, 0:D_RNN])
    xb = _dot(xb16, win_ref[:, D_RNN:2 * D_RNN])
    xpad = jnp.concatenate([st_ref[jj] for jj in range(CONV_W - 1)] + [xb], axis=0)
    xc = cb_ref[...]
    for jj in range(CONV_W):
        xc = xc + cw_ref[jj:jj + 1, :] * xpad[jj * bd:(jj + n_new) * bd]
    for jj in range(CONV_W - 1):
        newst_ref[jj] = xpad[(n_new + jj) * bd:(n_new + jj + 1) * bd]
    xc_bf = xc.astype(BF16)
    for j in range(N_GATE_TILES):
        lo = j * GATE_TILE
        w = min(GATE_TILE, D_RNN - lo)
        a, u = _rglru_gate_tile(j, xc, xc_bf, wband_ref, bg_ref, lam_ref)
        h = h0_ref[:, lo:lo + w]
        for t in range(n_new):
            h = a[t * bd:(t + 1) * bd] * h + u[t * bd:(t + 1) * bd]
            h_sc[t * bd:(t + 1) * bd, lo:lo + w] = h
        hlast_ref[:, lo:lo + w] = h
    mixed = (h_sc[...] * jax.nn.gelu(gate_br)).astype(BF16)
    y_ref[...] = _layer_norm(ALPHA * x_ref[...] + _dot(mixed, wout_ref[...]), g_ref[...], b_ref[...])


def rglru_sample(x_bf_tm, x_tm, state_tm, h0, w_in, conv_w, conv_b, wband, b_gate, lam, w_out, g, b):
    m, d = x_tm.shape
    bd = h0.shape[0]
    full = lambda shape: pl.BlockSpec(shape, lambda: (0,) * len(shape))
    args = (x_bf_tm, x_tm, state_tm, h0, w_in, conv_w, conv_b.reshape(1, D_RNN), wband, b_gate,
            lam.reshape(1, D_RNN), w_out, g.reshape(1, d), b.reshape(1, d))
    return pl.pallas_call(
        functools.partial(_rglru_sample_kernel, n_new=m // bd),
        in_specs=[full(a.shape) for a in args],
        out_specs=[full((m, d)), full((CONV_W - 1, bd, D_RNN)), full((bd, D_RNN))],
        out_shape=[jax.ShapeDtypeStruct((m, d), F32), jax.ShapeDtypeStruct((CONV_W - 1, bd, D_RNN), F32),
                   jax.ShapeDtypeStruct((bd, D_RNN), F32)],
        scratch_shapes=[pltpu.VMEM((m, D_RNN), F32)],
        compiler_params=pltpu.CompilerParams(vmem_limit_bytes=VMEM_LIMIT),
        name="rglru_sample",
    )(*args)


def _rope_table_kernel(cos_ref, sin_ref, *, pos0, period):
    n, half = cos_ref.shape
    r = lax.broadcasted_iota(jnp.int32, (n, half), 0)
    i = lax.broadcasted_iota(jnp.int32, (n, half), 1)
    pos = (pos0 + r % period).astype(F32)
    inv = jnp.exp(i.astype(F32) * (-math.log(ROPE_BASE) / half))
    ang = pos * inv
    cos_ref[...] = jnp.cos(ang)
    sin_ref[...] = jnp.sin(ang)


def rope_tables(n, pos0, period):
    half = DK_C // 2
    spec = pl.BlockSpec((n, half), lambda: (0, 0))
    return pl.pallas_call(
        functools.partial(_rope_table_kernel, pos0=pos0, period=period),
        out_specs=[spec, spec],
        out_shape=[jax.ShapeDtypeStruct((n, half), F32)] * 2,
        name="rope_tables",
    )()


def _ret_proj_kernel(x_ref, w_ref, cos_ref, sin_ref, q_ref, k_ref, v_ref, g_ref):
    x = x_ref[0]
    cos = cos_ref[...]
    sin = sin_ref[...]
    half = DK_C // 2
    qk_w = H_C * DK_C
    v_w = H_C * DV_C

    def roped(y, scale):
        outs = []
        for h in range(H_C):
            x1 = y[:, h * DK_C:h * DK_C + half]
            x2 = y[:, h * DK_C + half:(h + 1) * DK_C]
            outs += [x1 * cos - x2 * sin, x1 * sin + x2 * cos]
        return (jnp.concatenate(outs, axis=1) * scale).astype(BF16)

    q_ref[0] = roped(_dot(x, w_ref[:, 0:qk_w]), 1.0)
    k_ref[0] = roped(_dot(x, w_ref[:, qk_w:2 * qk_w]), DK_C ** -0.5)
    v_ref[0] = _dot(x, w_ref[:, 2 * qk_w:2 * qk_w + v_w]).astype(BF16)
    g_ref[0] = _dot(x, w_ref[:, 2 * qk_w + v_w:2 * qk_w + 2 * v_w])


def ret_proj(x_bf, w_in, cos, sin, tm):
    bsz, seq, d = x_bf.shape
    qk_w, v_w = H_C * DK_C, H_C * DV_C
    xmap = lambda b, i: (b, i, 0)
    tmap = lambda b, i: (i, 0)
    return pl.pallas_call(
        _ret_proj_kernel,
        grid=(bsz, seq // tm),
        in_specs=[pl.BlockSpec((1, tm, d), xmap), _const_spec((d, 2 * qk_w + 2 * v_w)),
                  pl.BlockSpec((tm, DK_C // 2), tmap), pl.BlockSpec((tm, DK_C // 2), tmap)],
        out_specs=[pl.BlockSpec((1, tm, qk_w), xmap), pl.BlockSpec((1, tm, qk_w), xmap),
                   pl.BlockSpec((1, tm, v_w), xmap), pl.BlockSpec((1, tm, v_w), xmap)],
        out_shape=[jax.ShapeDtypeStruct((bsz, seq, qk_w), BF16), jax.ShapeDtypeStruct((bsz, seq, qk_w), BF16),
                   jax.ShapeDtypeStruct((bsz, seq, v_w), BF16), jax.ShapeDtypeStruct((bsz, seq, v_w), F32)],
        compiler_params=_params("parallel", "parallel"),
        name="ret_proj",
    )(x_bf, w_in, cos, sin)


def _retention_kernel(lg_ref, q_ref, k_ref, v_ref, g_ref, gng_ref, gnb_ref, s0_ref, o_ref, sout_ref, s_sc,
                      *, t_true):
    c = pl.program_id(2)

    @pl.when(c == 0)
    def _load_state():
        s_sc[...] = s0_ref[...]

    lg = lg_ref[:, 0:1]
    q = q_ref[0]
    k = k_ref[0]
    v = v_ref[0]
    t = q.shape[0]
    ti = lax.broadcasted_iota(jnp.int32, (t, t), 0)
    si = lax.broadcasted_iota(jnp.int32, (t, t), 1)
    diff = (ti - si).astype(F32)
    decay = jnp.where(diff >= 0, jnp.exp(jnp.maximum(diff, 0.0) * lg), 0.0)
    scores = lax.dot_general(q, k, _NT, preferred_element_type=F32) * decay
    o = _dot(scores.astype(BF16), v)
    state = s_sc[...]
    tcol = lax.broadcasted_iota(jnp.int32, (t, 1), 0).astype(F32)
    o = o + _dot(q, state.astype(BF16)) * jnp.exp((tcol + 1.0) * lg)
    zeta = jnp.exp((t_true - 1.0 - tcol) * lg)
    kz = (k.astype(F32) * zeta).astype(BF16)
    new_state = jnp.exp(t_true * lg) * state + lax.dot_general(kz, v, _TN, preferred_element_type=F32)
    s_sc[...] = new_state

    mu = jnp.mean(o, axis=-1, keepdims=True)
    oc = o - mu
    var = jnp.mean(oc * oc, axis=-1, keepdims=True)
    on = oc * lax.rsqrt(var + GN_EPS) * gng_ref[...] + gnb_ref[...]
    o_ref[0] = (jax.nn.silu(g_ref[0]) * on).astype(BF16)

    @pl.when(c == pl.num_programs(2) - 1)
    def _store_state():
        sout_ref[...] = new_state


def retention(log_g, q, k, v, g, gn_g, gn_b, s0, chunk, t_true):
    bsz, seq, _ = q.shape
    v_w = H_C * DV_C
    qmap = lambda b, h, c: (b, c, h)
    smap = lambda b, h, c: (b, h, 0, 0)
    hmap = lambda b, h, c: (0, h)
    return pl.pallas_call(
        functools.partial(_retention_kernel, t_true=float(t_true)),
        grid=(bsz, H_C, seq // chunk),
        in_specs=[pl.BlockSpec((None, 1, LANES), lambda b, h, c: (h, 0, 0)),
                  pl.BlockSpec((1, chunk, DK_C), qmap), pl.BlockSpec((1, chunk, DK_C), qmap),
                  pl.BlockSpec((1, chunk, DV_C), qmap), pl.BlockSpec((1, chunk, DV_C), qmap),
                  pl.BlockSpec((1, DV_C), hmap), pl.BlockSpec((1, DV_C), hmap),
                  pl.BlockSpec((None, None, DK_C, DV_C), smap)],
        out_specs=[pl.BlockSpec((1, chunk, DV_C), qmap), pl.BlockSpec((None, None, DK_C, DV_C), smap)],
        out_shape=[jax.ShapeDtypeStruct((bsz, seq, v_w), BF16),
                   jax.ShapeDtypeStruct((bsz, H_C, DK_C, DV_C), F32)],
        scratch_shapes=[pltpu.VMEM((DK_C, DV_C), F32)],
        compiler_params=_params("parallel", "parallel", "arbitrary"),
        name="retention",
    )(log_g, q, k, v, g, gn_g.reshape(1, v_w), gn_b.reshape(1, v_w), s0)


TM_DENSE = 512
TQ_ATTN = 512
T_RGLRU = 512
PAGE_GROUP = 8


def kernel(x_prompt, x_sample, cache_k_a, cache_v_a, cache_logf_a, page_table, state_conv_b, state_h_b, state_ret_c, w_in_a, b_f_a, w_out_a, w_in_b, conv_w_b, conv_b_b, w_gate_b, b_gate_b, lam_b, w_out_b, w_in_c, gn_g_c, gn_b_c, w_out_c, ln1_g, ln1_b, w_up_ffn, w_down_ffn, ln2_g, ln2_b):
    bsz, seq, d = x_prompt.shape
    bd, n_new, _ = x_sample.shape
    mp, ms = bsz * seq, bd * n_new
    past_len = page_table.shape[1] * cache_k_a.shape[2]

    xp = x_prompt.reshape(mp, d)
    xs = x_sample.reshape(ms, d)
    xp_bf = xp.astype(BF16)
    xs_bf = xs.astype(BF16)

    n_la, pool, page = cache_k_a.shape[:3]
    kT_pool = jnp.transpose(cache_k_a, (0, 1, 3, 4, 2)).reshape(n_la, pool, d, page)
    vT_pool = jnp.transpose(cache_v_a, (0, 1, 3, 4, 2)).reshape(n_la, pool, d, page)
    lfT_pool = jnp.transpose(cache_logf_a, (0, 1, 3, 2))

    log_g = jnp.log1p(-jnp.exp2(-5.0 - jnp.arange(H_C, dtype=F32)))
    log_g = jnp.broadcast_to(log_g[:, None, None], (H_C, 1, LANES))

    outs_a = {name: [] for name in ("kp", "vp", "lfp", "ks", "vs", "lfs")}
    out_b, out_c = {}, {}

    for i in range(DEPTH):
        jl, kind = divmod(i, 3)
        if kind == 0:
            w = w_in_a[jl]
            wq = w[:, 0:d].astype(BF16)
            wkT = w[:, d:2 * d].T.astype(BF16)
            wvT = w[:, 2 * d:3 * d].T.astype(BF16)
            wfT = w[:, 3 * d:].T.astype(BF16)
            w_out = w_out_a[jl].astype(BF16)
            q_bf, kT, vT, kT_bf, vT_bf, lfT = fox_proj(
                xp_bf.reshape(bsz, seq, d), wq, wkT, wvT, wfT, b_f_a[jl].reshape(H_A, 1), TM_DENSE)
            ct = fox_cumsum(lfT)
            o_bf = fox_attn(q_bf, kT_bf, vT_bf, jnp.transpose(ct, (0, 2, 1)), ct, TQ_ATTN)
            mix_p = (o_bf.reshape(mp, d), w_out)
            outs_a["kp"].append(kT)
            outs_a["vp"].append(vT)
            outs_a["lfp"].append(lfT)
            wf_pad = jnp.pad(w[:, 3 * d:], ((0, 0), (0, LANES - H_A))).astype(BF16)
            bf_pad = jnp.pad(b_f_a[jl], (0, LANES - H_A)).reshape(1, LANES)
            qs, ks, vs, lfs = fox_proj_sample(xs_bf, w[:, 0:3 * d].astype(BF16), wf_pad, bf_pad)
            lfs = lfs[:, :H_A].reshape(bd, n_new, H_A)
            lfs_T = jnp.pad(jnp.transpose(lfs, (0, 2, 1)), ((0, 0), (0, 0), (0, LANES - n_new)))
            o_s = fox_sample_attn(page_table, kT_pool, vT_pool, lfT_pool, jl, qs.reshape(bd, n_new, d),
                                  ks.reshape(bd, n_new, d), vs.reshape(bd, n_new, d), lfs_T, PAGE_GROUP)
            mix_s = (o_s.reshape(ms, d).astype(BF16), w_out)
            outs_a["ks"].append(ks.reshape(bd, n_new, H_A, DH_A))
            outs_a["vs"].append(vs.reshape(bd, n_new, H_A, DH_A))
            outs_a["lfs"].append(lfs)
        elif kind == 1:
            w_in = w_in_b[jl].astype(BF16)
            w_out = w_out_b[jl].astype(BF16)
            wband = _band_gate_weights(w_gate_b[jl])
            xp3, xp_bf3, tail, hlast = rglru_prompt(
                xp_bf.reshape(bsz, seq, d), xp.reshape(bsz, seq, d), w_in, conv_w_b[jl], conv_b_b[jl], wband,
                b_gate_b[jl], lam_b[jl], w_out, ln1_g[i], ln1_b[i], T_RGLRU)
            xp, xp_bf = xp3.reshape(mp, d), xp_bf3.reshape(mp, d)
            out_b["cp"] = tail[:, SUBLANES - (CONV_W - 1):, :]
            out_b["hp"] = hlast[:, 0, :]
            to_tm = lambda a: jnp.transpose(a.reshape(bd, n_new, d), (1, 0, 2)).reshape(ms, d)
            ys_tm, newst, hs = rglru_sample(
                to_tm(xs_bf), to_tm(xs), jnp.transpose(state_conv_b[jl], (1, 0, 2)), state_h_b[jl], w_in,
                conv_w_b[jl], conv_b_b[jl], wband, b_gate_b[jl], lam_b[jl], w_out, ln1_g[i], ln1_b[i])
            xs = jnp.transpose(ys_tm.reshape(n_new, bd, d), (1, 0, 2)).reshape(ms, d)
            xs_bf = xs.astype(BF16)
            out_b["cs"] = jnp.transpose(newst, (1, 0, 2))
            out_b["hs"] = hs
            mix_p = mix_s = None
        else:
            w_in = w_in_c[jl].astype(BF16)
            w_out = w_out_c[jl].astype(BF16)
            v_w = H_C * DV_C
            cos_p, sin_p = rope_tables(seq, 0, seq)
            q, k, v, g = ret_proj(xp_bf.reshape(bsz, seq, d), w_in, cos_p, sin_p, TM_DENSE)
            chunk = CHUNK_C if seq % CHUNK_C == 0 else seq
            og, s_p = retention(log_g, q, k, v, g, gn_g_c[jl], gn_b_c[jl],
                                jnp.zeros((bsz, H_C, DK_C, DV_C), F32), chunk, chunk)
            mix_p = (og.reshape(mp, v_w), w_out)
            out_c["sp"] = s_p
            cos_s, sin_s = rope_tables(ms, past_len, n_new)
            q, k, v, g = ret_proj(xs_bf.reshape(1, ms, d), w_in, cos_s, sin_s, ms)
            pad_rows = lambda a: jnp.pad(a.reshape(bd, n_new, a.shape[-1]), ((0, 0), (0, SUBLANES - n_new), (0, 0)))
            og, s_s = retention(log_g, pad_rows(q), pad_rows(k), pad_rows(v), pad_rows(g), gn_g_c[jl], gn_b_c[jl],
                                state_ret_c[jl], SUBLANES, n_new)
            mix_s = (og[:, :n_new].reshape(ms, v_w), w_out)
            out_c["ss"] = s_s
        if mix_p is not None:
            xp, xp_bf = proj_ln(mix_p[0], mix_p[1], xp, ln1_g[i], ln1_b[i], TM_DENSE)
            xs, xs_bf = proj_ln(mix_s[0], mix_s[1], xs, ln1_g[i], ln1_b[i], TM_DENSE)
        w_up = w_up_ffn[i].astype(BF16)
        w_dn = w_down_ffn[i].astype(BF16)
        xp, xp_bf = ffn_ln(xp_bf, xp, w_up, w_dn, ln2_g[i], ln2_b[i], TM_DENSE)
        xs, xs_bf = ffn_ln(xs_bf, xs, w_up, w_dn, ln2_g[i], ln2_b[i], TM_DENSE)

    kv_out = lambda lst: jnp.transpose(jnp.stack(lst).reshape(n_la, bsz, H_A, DH_A, seq), (0, 1, 4, 2, 3))
    return (xp.reshape(bsz, seq, d), xs.reshape(bd, n_new, d),
            kv_out(outs_a["kp"]), kv_out(outs_a["vp"]), jnp.transpose(jnp.stack(outs_a["lfp"]), (0, 1, 3, 2)),
            jnp.stack(outs_a["ks"]), jnp.stack(outs_a["vs"]), jnp.stack(outs_a["lfs"]),
            out_b["cp"][None], out_b["hp"][None], out_b["cs"][None], out_b["hs"][None],
            out_c["sp"][None], out_c["ss"][None])
```

```python
import functools
import math

import jax
import jax.numpy as jnp
from jax import lax
from jax.experimental import pallas as pl
from jax.experimental.pallas import tpu as pltpu

F32 = jnp.float32
BF16 = jnp.bfloat16

D_MODEL = 1024
DEPTH = 4
H_A = 16
DH_A = D_MODEL // H_A
D_RNN = 1408
N_BLK_B = 16
BW_B = D_RNN // N_BLK_B
CONV_W = 4
LRU_C = 8.0
H_C = 4
DK_C = D_MODEL // H_C
DV_C = 2 * DK_C
CHUNK_C = 128
ROPE_BASE = 10000.0
D_FF = 2816
ALPHA = (2 * DEPTH) ** 0.25
LN_EPS = 1e-5
GN_EPS = 1e-5
LOG2E = math.log2(math.e)

LANES = 128
SUBLANES = 8
MXU_COLS = 256
VMEM_LIMIT = 56 * 1024 * 1024

_NT = (((1,), (1,)), ((), ()))
_TN = (((0,), (0,)), ((), ()))


def _params(*sem):
    return pltpu.CompilerParams(dimension_semantics=sem, vmem_limit_bytes=VMEM_LIMIT)


def _const_spec(shape):
    zeros = (0,) * len(shape)
    return pl.BlockSpec(shape, lambda *_: zeros, pipeline_mode=pl.Buffered(1))


def _dot(a, b):
    return jnp.dot(a, b, preferred_element_type=F32)


def _layer_norm(z, g, b):
    mu = jnp.mean(z, axis=-1, keepdims=True)
    zc = z - mu
    var = jnp.mean(zc * zc, axis=-1, keepdims=True)
    return zc * lax.rsqrt(var + LN_EPS) * g + b


def _log_sigmoid(z):
    return jnp.minimum(z, 0.0) - jnp.log1p(jnp.exp(-jnp.abs(z)))


def _dot_f32_by_01(x, m01):
    hi = x.astype(BF16)
    r1 = x - hi.astype(F32)
    mid = r1.astype(BF16)
    lo = (r1 - mid.astype(F32)).astype(BF16)
    return _dot(hi, m01) + _dot(mid, m01) + _dot(lo, m01)


def _tri01(n, strict_lower_rows):
    r = lax.broadcasted_iota(jnp.int32, (n, n), 0)
    c = lax.broadcasted_iota(jnp.int32, (n, n), 1)
    keep = (r > c) if strict_lower_rows else (r <= c)
    return jnp.where(keep, 1.0, 0.0).astype(BF16)


def _proj_ln_kernel(a_ref, w_ref, x_ref, g_ref, b_ref, y_ref, ybf_ref):
    m = _dot(a_ref[...], w_ref[...])
    y = _layer_norm(ALPHA * x_ref[...] + m, g_ref[...], b_ref[...])
    y_ref[...] = y
    ybf_ref[...] = y.astype(BF16)


def proj_ln(a_bf, w_bf, x, g, b, tm):
    m, k = a_bf.shape
    d = w_bf.shape[1]
    tm = min(tm, m)
    row = lambda i: (i, 0)
    return pl.pallas_call(
        _proj_ln_kernel,
        grid=(m // tm,),
        in_specs=[pl.BlockSpec((tm, k), row), _const_spec((k, d)), pl.BlockSpec((tm, d), row),
                  _const_spec((1, d)), _const_spec((1, d))],
        out_specs=[pl.BlockSpec((tm, d), row), pl.BlockSpec((tm, d), row)],
        out_shape=[jax.ShapeDtypeStruct((m, d), F32), jax.ShapeDtypeStruct((m, d), BF16)],
        compiler_params=_params("parallel"),
        name="proj_ln",
    )(a_bf, w_bf, x, g.reshape(1, d), b.reshape(1, d))


def _ffn_ln_kernel(xbf_ref, x_ref, wup_ref, wdn_ref, g_ref, b_ref, y_ref, ybf_ref, *, n_chunks):
    xb = xbf_ref[...]
    cw = D_FF // n_chunks
    acc = None
    for c in range(n_chunks):
        gate = _dot(xb, wup_ref[:, c * cw:(c + 1) * cw])
        up = _dot(xb, wup_ref[:, D_FF + c * cw:D_FF + (c + 1) * cw])
        h = (jax.nn.silu(gate) * up).astype(BF16)
        part = _dot(h, wdn_ref[c * cw:(c + 1) * cw, :])
        acc = part if acc is None else acc + part
    y = _layer_norm(ALPHA * x_ref[...] + acc, g_ref[...], b_ref[...])
    y_ref[...] = y
    ybf_ref[...] = y.astype(BF16)


def ffn_ln(x_bf, x, w_up_bf, w_dn_bf, g, b, tm, n_chunks=1):
    m, d = x.shape
    tm = min(tm, m)
    row = lambda i: (i, 0)
    return pl.pallas_call(
        functools.partial(_ffn_ln_kernel, n_chunks=n_chunks),
        grid=(m // tm,),
        in_specs=[pl.BlockSpec((tm, d), row), pl.BlockSpec((tm, d), row),
                  _const_spec((d, 2 * D_FF)), _const_spec((D_FF, d)),
                  _const_spec((1, d)), _const_spec((1, d))],
        out_specs=[pl.BlockSpec((tm, d), row), pl.BlockSpec((tm, d), row)],
        out_shape=[jax.ShapeDtypeStruct((m, d), F32), jax.ShapeDtypeStruct((m, d), BF16)],
        compiler_params=_params("parallel"),
        name="ffn_ln",
    )(x_bf, x, w_up_bf, w_dn_bf, g.reshape(1, d), b.reshape(1, d))


def _fox_proj_kernel(*refs, layer, n_layers, first):
    if first:
        x_ref, wq_ref, wkT_ref, wvT_ref, wfT_ref, bf_ref = refs[:6]
    else:
        x_ref, wq_ref, wkT_ref, wvT_ref, wfT_ref, bf_ref, _, _ = refs[:8]
    q_ref, kT_ref, vT_ref, kTb_ref, vTb_ref, lfT_ref = refs[-6:]
    x = x_ref[0]
    q_ref[0] = (_dot(x, wq_ref[...]) * (DH_A ** -0.5 * LOG2E)).astype(BF16)
    kT = lax.dot_general(wkT_ref[...], x, _NT, preferred_element_type=F32)
    vT = lax.dot_general(wvT_ref[...], x, _NT, preferred_element_type=F32)
    if first:
        for l in range(n_layers):
            kT_ref[l, 0] = kT if l == layer else jnp.zeros_like(kT)
            vT_ref[l, 0] = vT if l == layer else jnp.zeros_like(vT)
    else:
        kT_ref[0] = kT
        vT_ref[0] = vT
    kTb_ref[0] = kT.astype(BF16)
    vTb_ref[0] = vT.astype(BF16)
    f = lax.dot_general(wfT_ref[...], x, _NT, preferred_element_type=F32)
    lfT_ref[0] = _log_sigmoid(f + bf_ref[...])


def fox_proj(x_bf, wq, wkT, wvT, wfT, bf_col, tm, layer, n_layers, kv_all=None):
    bsz, seq, d = x_bf.shape
    first = kv_all is None
    xmap = lambda b, i: (b, i, 0)
    tmap = lambda b, i: (b, 0, i)
    if first:
        kv_spec = pl.BlockSpec((n_layers, 1, d, tm), lambda b, i: (0, b, 0, i))
        extra_in, extra_specs, aliases = (), [], {}
    else:
        kv_spec = pl.BlockSpec((None, 1, d, tm), lambda b, i: (layer, b, 0, i))
        extra_in = tuple(kv_all)
        extra_specs = [pl.BlockSpec(memory_space=pl.ANY)] * 2
        aliases = {6: 1, 7: 2}
    kv_shape = jax.ShapeDtypeStruct((n_layers, bsz, d, seq), F32)
    return pl.pallas_call(
        functools.partial(_fox_proj_kernel, layer=layer, n_layers=n_layers, first=first),
        grid=(bsz, seq // tm),
        in_specs=[pl.BlockSpec((1, tm, d), xmap), _const_spec((d, d)), _const_spec((d, d)),
                  _const_spec((d, d)), _const_spec((H_A, d)), _const_spec((H_A, 1))] + extra_specs,
        out_specs=[pl.BlockSpec((1, tm, d), xmap), kv_spec, kv_spec, pl.BlockSpec((1, d, tm), tmap),
                   pl.BlockSpec((1, d, tm), tmap), pl.BlockSpec((1, H_A, tm), tmap)],
        out_shape=[jax.ShapeDtypeStruct((bsz, seq, d), BF16), kv_shape, kv_shape,
                   jax.ShapeDtypeStruct((bsz, d, seq), BF16), jax.ShapeDtypeStruct((bsz, d, seq), BF16),
                   jax.ShapeDtypeStruct((bsz, H_A, seq), F32)],
        input_output_aliases=aliases,
        compiler_params=_params("parallel", "parallel"),
        name="fox_proj",
    )(x_bf, wq, wkT, wvT, wfT, bf_col, *extra_in)


def _fox_cumsum_kernel(lf_ref, ct_ref, *, chunk):
    seq = lf_ref.shape[2]
    prefix = _tri01(chunk, strict_lower_rows=False)
    carry = jnp.zeros((H_A, 1), F32)
    for i in range(seq // chunk):
        cs = _dot_f32_by_01(lf_ref[0, :, i * chunk:(i + 1) * chunk], prefix) + carry
        ct_ref[0, :, i * chunk:(i + 1) * chunk] = cs
        carry = cs[:, chunk - 1:chunk]


def fox_cumsum(lfT):
    bsz, h, seq = lfT.shape
    chunk = min(seq, MXU_COLS)
    spec = pl.BlockSpec((1, h, seq), lambda b: (b, 0, 0))
    return pl.pallas_call(
        functools.partial(_fox_cumsum_kernel, chunk=chunk),
        grid=(bsz,), in_specs=[spec], out_specs=spec,
        out_shape=jax.ShapeDtypeStruct((bsz, h, seq), F32),
        compiler_params=_params("parallel"),
        name="fox_cumsum",
    )(lfT)


def _fox_attn_kernel(qi_ref, kj_ref, q_ref, kT_ref, vT_ref, c_ref, ct_ref, o_ref, cq_sc, m_sc, l_sc, acc_sc,
                     *, tq, n_pairs):
    g = pl.program_id(1)
    i = qi_ref[pl.program_id(2)]
    j = kj_ref[pl.program_id(2)]
    tk = kT_ref.shape[2]
    lane = lax.broadcasted_iota(jnp.int32, (tq, LANES), 1)
    first_head = lane < DH_A

    def spread(x):
        return jnp.concatenate([x] * (tk // LANES), axis=1)

    @pl.when(j == 0)
    def _init():
        cblk = c_ref[0]
        hl = lax.broadcasted_iota(jnp.int32, cblk.shape, 1)
        for hh in range(2 * n_pairs):
            cq = jnp.sum(jnp.where(hl == 2 * n_pairs * g + hh, cblk, 0.0), axis=1, keepdims=True)
            cq_sc[hh] = jnp.broadcast_to(cq * LOG2E, (tq, LANES))
        m_sc[...] = jnp.full(m_sc.shape, -jnp.inf, F32)
        l_sc[...] = jnp.zeros(l_sc.shape, F32)
        acc_sc[...] = jnp.zeros(acc_sc.shape, F32)

    def step(diagonal):
        for pp in range(n_pairs):
            q = q_ref[0, :, pp * LANES:(pp + 1) * LANES]
            kT = kT_ref[0, pp * LANES:(pp + 1) * LANES, :]
            vT = vT_ref[0, pp * LANES:(pp + 1) * LANES, :]
            alphas, pvs = [], []
            for a in range(2):
                hh = 2 * pp + a
                qa = jnp.where(first_head if a == 0 else jnp.logical_not(first_head), q, jnp.zeros_like(q))
                ck = ct_ref[0, pl.ds(2 * n_pairs * g + hh, 1), :] * LOG2E
                y = _dot(qa, kT) - ck
                if diagonal:
                    r = lax.broadcasted_iota(jnp.int32, y.shape, 0)
                    c = lax.broadcasted_iota(jnp.int32, y.shape, 1)
                    y = jnp.where(r >= c, y, -jnp.inf)
                cq = cq_sc[hh]
                m_old = m_sc[hh]
                m_new = jnp.maximum(m_old, jnp.max(y, axis=1, keepdims=True) + cq)
                alpha = jnp.exp2(m_old - m_new)
                pr = jnp.exp2(y - spread(m_new - cq))
                l_sc[hh] = alpha * l_sc[hh] + jnp.sum(pr, axis=1, keepdims=True)
                m_sc[hh] = m_new
                pvs.append(lax.dot_general(pr.astype(BF16), vT, _NT, preferred_element_type=F32))
                alphas.append(alpha)
            acc_sc[pp] = (acc_sc[pp] * jnp.where(first_head, alphas[0], alphas[1])
                          + jnp.where(first_head, pvs[0], pvs[1]))

    @pl.when(j < i)
    def _below():
        step(False)

    @pl.when(j == i)
    def _diag():
        step(True)
        for pp in range(n_pairs):
            norm = jnp.where(first_head, l_sc[2 * pp], l_sc[2 * pp + 1])
            o_ref[0, :, pp * LANES:(pp + 1) * LANES] = (acc_sc[pp] / norm).astype(BF16)


def fox_attn(q_bf, kT_bf, vT_bf, c, ct, tq, n_pairs):
    bsz, seq, d = q_bf.shape
    nq = seq // tq
    wid = n_pairs * LANES
    pairs = [(i, j) for i in range(nq) for j in range(i + 1)]
    qi = jnp.asarray([ij[0] for ij in pairs], jnp.int32)
    kj = jnp.asarray([ij[1] for ij in pairs], jnp.int32)
    qmap = lambda b, g, s, qi, kj: (b, qi[s], g)
    kmap = lambda b, g, s, qi, kj: (b, g, kj[s])
    grid_spec = pltpu.PrefetchScalarGridSpec(
        num_scalar_prefetch=2,
        grid=(bsz, d // wid, len(pairs)),
        in_specs=[pl.BlockSpec((1, tq, wid), qmap), pl.BlockSpec((1, wid, tq), kmap),
                  pl.BlockSpec((1, wid, tq), kmap),
                  pl.BlockSpec((1, tq, H_A), lambda b, g, s, qi, kj: (b, qi[s], 0)),
                  pl.BlockSpec((1, H_A, tq), lambda b, g, s, qi, kj: (b, 0, kj[s]))],
        out_specs=pl.BlockSpec((1, tq, wid), qmap),
        scratch_shapes=[pltpu.VMEM((2 * n_pairs, tq, LANES), F32), pltpu.VMEM((2 * n_pairs, tq, LANES), F32),
                        pltpu.VMEM((2 * n_pairs, tq, LANES), F32), pltpu.VMEM((n_pairs, tq, LANES), F32)],
    )
    return pl.pallas_call(
        functools.partial(_fox_attn_kernel, tq=tq, n_pairs=n_pairs),
        grid_spec=grid_spec,
        out_shape=jax.ShapeDtypeStruct((bsz, seq, d), BF16),
        compiler_params=_params("parallel", "parallel", "arbitrary"),
        name="fox_attn",
    )(qi, kj, q_bf, kT_bf, vT_bf, c, ct)


def _fox_proj_sample_kernel(x_ref, w_ref, wf_ref, bf_ref, q_ref, k_ref, v_ref, lf_ref):
    x = x_ref[...]
    d = x.shape[1]
    q_ref[...] = _dot(x, w_ref[:, 0:d])
    k_ref[...] = _dot(x, w_ref[:, d:2 * d])
    v_ref[...] = _dot(x, w_ref[:, 2 * d:3 * d])
    lf_ref[...] = _log_sigmoid(_dot(x, wf_ref[...]) + bf_ref[...])


def fox_proj_sample(x_bf, w_qkv, wf_pad, bf_pad):
    m, d = x_bf.shape
    full = lambda shape: pl.BlockSpec(shape, lambda: (0,) * len(shape))
    return pl.pallas_call(
        _fox_proj_sample_kernel,
        in_specs=[full((m, d)), full((d, 3 * d)), full((d, LANES)), full((1, LANES))],
        out_specs=[full((m, d)), full((m, d)), full((m, d)), full((m, LANES))],
        out_shape=[jax.ShapeDtypeStruct((m, d), F32)] * 3 + [jax.ShapeDtypeStruct((m, LANES), F32)],
        compiler_params=pltpu.CompilerParams(vmem_limit_bytes=VMEM_LIMIT),
        name="fox_proj_sample",
    )(x_bf, w_qkv, wf_pad, bf_pad)


def _fox_sample_attn_kernel(pt_ref, *refs, n_new, group):
    kT_refs, vT_refs, lf_refs = refs[0:group], refs[group:2 * group], refs[2 * group:3 * group]
    (q_ref, kn_ref, vn_ref, lfs_ref, o_ref,
     qbd_sc, ctn_sc, ctcol_sc, m_sc, l_sc, acc_sc, carry_sc) = refs[3 * group:]
    j = pl.program_id(1)
    rows = n_new * H_A
    d = qbd_sc.shape[1]
    page = kT_refs[0].shape[1]

    def head_diag():
        r = lax.broadcasted_iota(jnp.int32, (rows, d), 0)
        c = lax.broadcasted_iota(jnp.int32, (rows, d), 1)
        return (c // DH_A) == (r % H_A)

    def spread(x, width):
        return jnp.concatenate([x] * (width // LANES), axis=1) if width >= LANES else x[:, 0:width]

    def online_update(s, pv_fn):
        m_old = m_sc[...]
        m_new = jnp.maximum(m_old, jnp.max(s, axis=1, keepdims=True))
        alpha = jnp.exp(m_old - m_new)
        pr = jnp.exp(s - spread(m_new, s.shape[1]))
        l_sc[...] = alpha * l_sc[...] + jnp.sum(pr, axis=1, keepdims=True)
        m_sc[...] = m_new
        acc_sc[...] = spread(alpha, d) * acc_sc[...] + pv_fn(pr.astype(BF16))

    @pl.when(j == 0)
    def _init():
        q = q_ref[0]
        qe = jnp.concatenate([jnp.broadcast_to(q[t:t + 1, :], (H_A, d)) for t in range(n_new)], axis=0)
        qbd_sc[...] = jnp.where(head_diag(), qe * DH_A ** -0.5, 0.0).astype(BF16)
        ctn = _dot_f32_by_01(lfs_ref[0], _tri01(LANES, strict_lower_rows=False))
        ctn_sc[...] = ctn
        ctcol = jnp.concatenate([ctn[:, t:t + 1] for t in range(n_new)], axis=0)
        ctcol_sc[...] = jnp.broadcast_to(ctcol, ctcol_sc.shape)
        m_sc[...] = jnp.full(m_sc.shape, -jnp.inf, F32)
        l_sc[...] = jnp.zeros(l_sc.shape, F32)
        acc_sc[...] = jnp.zeros(acc_sc.shape, F32)
        carry_sc[...] = jnp.zeros(carry_sc.shape, F32)

    qbd = qbd_sc[...]
    lf_all = jnp.concatenate([lf_refs[g][...] for g in range(group)], axis=0)
    suffix_all = _dot_f32_by_01(lf_all, _tri01(page, strict_lower_rows=True))
    carry = carry_sc[...]
    parts = []
    for g in range(group):
        bias = suffix_all[g * H_A:(g + 1) * H_A] + carry
        carry = carry + jnp.sum(lf_refs[g][...], axis=1, keepdims=True)
        parts.append(_dot(qbd, kT_refs[g][...].astype(BF16)) + jnp.concatenate([bias] * n_new, axis=0))
    carry_sc[...] = carry
    s = jnp.concatenate(parts, axis=1) + spread(ctcol_sc[...], group * page)

    def pv_pages(p):
        out = None
        for g in range(group):
            t = lax.dot_general(p[:, g * page:(g + 1) * page], vT_refs[g][...].astype(BF16), _NT,
                                preferred_element_type=F32)
            out = t if out is None else out + t
        return out

    online_update(s, pv_pages)

    @pl.when(j == pl.num_programs(1) - 1)
    def _finish():
        pad = jnp.zeros((SUBLANES - n_new, d), F32)
        kn = jnp.concatenate([kn_ref[0], pad], axis=0).astype(BF16)
        vn = jnp.concatenate([vn_ref[0], pad], axis=0).astype(BF16)
        sn = lax.dot_general(qbd_sc[...], kn, _NT, preferred_element_type=F32)
        ctk = jnp.concatenate([ctn_sc[:, 0:SUBLANES]] * n_new, axis=0)
        tq = lax.broadcasted_iota(jnp.int32, sn.shape, 0) // H_A
        tk = lax.broadcasted_iota(jnp.int32, sn.shape, 1)
        sn = jnp.where(tk <= tq, sn + (ctcol_sc[:, 0:SUBLANES] - ctk), -jnp.inf)
        online_update(sn, lambda p: _dot(p, vn))
        om = jnp.where(head_diag(), acc_sc[...] / spread(l_sc[...], d), 0.0)
        o_ref[0] = jnp.concatenate(
            [jnp.sum(om[t * H_A:(t + 1) * H_A], axis=0, keepdims=True) for t in range(n_new)], axis=0)


def fox_sample_attn(page_table, kT_pool, vT_pool, lfT_pool, layer, q, k_new, v_new, lfs_T, group):
    bd, n_new, d = q.shape
    n_pages = page_table.shape[1]
    page = kT_pool.shape[3]
    rows = n_new * H_A
    group = math.gcd(group, n_pages)
    pmaps = [lambda b, j, pt, g=g: (layer, pt[b, n_pages - 1 - (j * group + g)], 0, 0) for g in range(group)]
    bmap = lambda b, j, pt: (b, 0, 0)
    grid_spec = pltpu.PrefetchScalarGridSpec(
        num_scalar_prefetch=1,
        grid=(bd, n_pages // group),
        in_specs=([pl.BlockSpec((None, None, d, page), pm) for pm in pmaps]
                  + [pl.BlockSpec((None, None, d, page), pm) for pm in pmaps]
                  + [pl.BlockSpec((None, None, H_A, page), pm) for pm in pmaps]
                  + [pl.BlockSpec((1, n_new, d), bmap), pl.BlockSpec((1, n_new, d), bmap),
                     pl.BlockSpec((1, n_new, d), bmap), pl.BlockSpec((1, H_A, LANES), bmap)]),
        out_specs=pl.BlockSpec((1, n_new, d), bmap),
        scratch_shapes=[pltpu.VMEM((rows, d), BF16), pltpu.VMEM((H_A, LANES), F32),
                        pltpu.VMEM((rows, LANES), F32), pltpu.VMEM((rows, LANES), F32),
                        pltpu.VMEM((rows, LANES), F32), pltpu.VMEM((rows, d), F32), pltpu.VMEM((H_A, LANES), F32)],
    )
    return pl.pallas_call(
        functools.partial(_fox_sample_attn_kernel, n_new=n_new, group=group),
        grid_spec=grid_spec,
        out_shape=jax.ShapeDtypeStruct((bd, n_new, d), F32),
        compiler_params=_params("parallel", "arbitrary"),
        name="fox_sample_attn",
    )(page_table, *([kT_pool] * group), *([vT_pool] * group), *([lfT_pool] * group), q, k_new, v_new, lfs_T)


GATE_TILE = MXU_COLS
GATE_WIN = 2 * MXU_COLS
N_GATE_TILES = -(-D_RNN // GATE_TILE)


def _gate_window_start(j):
    first_block = (j * GATE_TILE) // BW_B
    return min((first_block * BW_B) // LANES * LANES, D_RNN - GATE_WIN)


def _band_gate_weights(w_gate):
    eye = jnp.eye(N_BLK_B, dtype=w_gate.dtype)
    tiles = []
    for part in range(2):
        wp = w_gate[:, :, part * BW_B:(part + 1) * BW_B]
        dense = (eye[:, None, :, None] * wp[:, :, None, :]).reshape(D_RNN, D_RNN)
        dense = jnp.pad(dense, ((0, 0), (0, N_GATE_TILES * GATE_TILE - D_RNN)))
        tiles.append(jnp.stack([
            dense[_gate_window_start(j):_gate_window_start(j) + GATE_WIN, j * GATE_TILE:(j + 1) * GATE_TILE]
            for j in range(N_GATE_TILES)]))
    return jnp.stack(tiles).astype(BF16)


def _rglru_gate_tile(j, xc, xc_bf, wband_ref, bg_ref, lam_ref):
    lo = j * GATE_TILE
    w = min(GATE_TILE, D_RNN - lo)
    ks = _gate_window_start(j)
    xw = xc_bf[:, ks:ks + GATE_WIN]
    r = jax.nn.sigmoid(_dot(xw, wband_ref[0, j])[:, :w] + bg_ref[0:1, lo:lo + w])
    ig = jax.nn.sigmoid(_dot(xw, wband_ref[1, j])[:, :w] + bg_ref[1:2, lo:lo + w])
    log_a = LRU_C * r * _log_sigmoid(lam_ref[:, lo:lo + w])
    a = jnp.exp(log_a)
    u = jnp.sqrt(-jnp.tanh(log_a) * (a * a + 1.0)) * (ig * xc[:, lo:lo + w])
    return a, u


def _rglru_prompt_kernel(xbf_ref, x_ref, win_ref, cw_ref, cb_ref, wband_ref, bg_ref, lam_ref, wout_ref,
                         g_ref, b_ref, y_ref, ybf_ref, tail_ref, hlast_ref, tail_sc, h_sc, a_sc, u_sc):
    t_rows = xbf_ref.shape[1]

    @pl.when(pl.program_id(1) == 0)
    def _reset():
        tail_sc[...] = jnp.zeros(tail_sc.shape, F32)
        h_sc[...] = jnp.zeros(h_sc.shape, F32)

    xb16 = xbf_ref[0]
    gate_br = _dot(xb16, win_ref[:, 0:D_RNN])
    xb = _dot(xb16, win_ref[:, D_RNN:2 * D_RNN])

    row8 = lax.broadcasted_iota(jnp.int32, (SUBLANES, D_RNN), 0)
    tail = tail_sc[...]
    xc = cb_ref[...] + cw_ref[CONV_W - 1:CONV_W, :] * xb
    for k in range(1, CONV_W):
        xs = pltpu.roll(xb, k, axis=0)
        head = jnp.where(row8 < k, pltpu.roll(tail, k, axis=0), xs[0:SUBLANES])
        xs = jnp.concatenate([head, xs[SUBLANES:]], axis=0)
        xc = xc + cw_ref[CONV_W - 1 - k:CONV_W - k, :] * xs
    new_tail = xb[t_rows - SUBLANES:t_rows]
    tail_sc[...] = new_tail
    tail_ref[0] = new_tail
    xc_bf = xc.astype(BF16)

    for j in range(N_GATE_TILES):
        lo = j * GATE_TILE
        w = min(GATE_TILE, D_RNN - lo)
        a, u = _rglru_gate_tile(j, xc, xc_bf, wband_ref, bg_ref, lam_ref)
        groups = (t_rows // SUBLANES, SUBLANES, w)
        a = a.reshape(groups)
        u = u.reshape(groups)
        pos = lax.broadcasted_iota(jnp.int32, groups, 1)
        for s in (1, 2, 4):
            live = pos >= s
            u = jnp.where(live, a * pltpu.roll(u, s, axis=1) + u, u)
            a = jnp.where(live, a * pltpu.roll(a, s, axis=1), a)
        a_sc[:, lo:lo + w] = a.reshape(t_rows, w)
        u_sc[:, lo:lo + w] = u.reshape(t_rows, w)

        def group(i, h_prev, lo=lo, w=w):
            blk = pl.ds(pl.multiple_of(i * SUBLANES, SUBLANES), SUBLANES)
            hb = a_sc[blk, lo:lo + w] * h_prev + u_sc[blk, lo:lo + w]
            u_sc[blk, lo:lo + w] = hb
            return hb[SUBLANES - 1:SUBLANES, :]

        h_sc[:, lo:lo + w] = lax.fori_loop(0, t_rows // SUBLANES, group, h_sc[:, lo:lo + w])

    hlast_ref[0] = jnp.broadcast_to(h_sc[...], (SUBLANES, D_RNN))
    mixed = (u_sc[...] * jax.nn.gelu(gate_br)).astype(BF16)
    y = _layer_norm(ALPHA * x_ref[0] + _dot(mixed, wout_ref[...]), g_ref[...], b_ref[...])
    y_ref[0] = y
    ybf_ref[0] = y.astype(BF16)


def rglru_prompt(x_bf, x, w_in, conv_w, conv_b, wband, b_gate, lam, w_out, g, b, t_rows):
    bsz, seq, d = x.shape
    xmap = lambda bi, ti: (bi, ti, 0)
    smap = lambda bi, ti: (bi, 0, 0)
    return pl.pallas_call(
        _rglru_prompt_kernel,
        grid=(bsz, seq // t_rows),
        in_specs=[pl.BlockSpec((1, t_rows, d), xmap), pl.BlockSpec((1, t_rows, d), xmap),
                  _const_spec((d, 2 * D_RNN)), _const_spec((CONV_W, D_RNN)), _const_spec((1, D_RNN)),
                  _const_spec((2, N_GATE_TILES, GATE_WIN, GATE_TILE)), _const_spec((2, D_RNN)),
                  _const_spec((1, D_RNN)), _const_spec((D_RNN, d)), _const_spec((1, d)), _const_spec((1, d))],
        out_specs=[pl.BlockSpec((1, t_rows, d), xmap), pl.BlockSpec((1, t_rows, d), xmap),
                   pl.BlockSpec((1, SUBLANES, D_RNN), smap), pl.BlockSpec((1, SUBLANES, D_RNN), smap)],
        out_shape=[jax.ShapeDtypeStruct((bsz, seq, d), F32), jax.ShapeDtypeStruct((bsz, seq, d), BF16),
                   jax.ShapeDtypeStruct((bsz, SUBLANES, D_RNN), F32),
                   jax.ShapeDtypeStruct((bsz, SUBLANES, D_RNN), F32)],
        scratch_shapes=[pltpu.VMEM((SUBLANES, D_RNN), F32), pltpu.VMEM((1, D_RNN), F32),
                        pltpu.VMEM((t_rows, D_RNN), F32), pltpu.VMEM((t_rows, D_RNN), F32)],
        compiler_params=_params("parallel", "arbitrary"),
        name="rglru_prompt",
    )(x_bf, x, w_in, conv_w, conv_b.reshape(1, D_RNN), wband, b_gate, lam.reshape(1, D_RNN), w_out,
      g.reshape(1, d), b.reshape(1, d))


def _rglru_sample_kernel(xbf_ref, x_ref, st_ref, h0_ref, win_ref, cw_ref, cb_ref, wband_ref, bg_ref, lam_ref,
                         wout_ref, g_ref, b_ref, y_ref, newst_ref, hlast_ref, h_sc, *, n_new):
    bd = h0_ref.shape[0]
    xb16 = xbf_ref[...]
    gate_br = _dot(xb16, win_ref[:, 0:D_RNN])
    xb = _dot(xb16, win_ref[:, D_RNN:2 * D_RNN])
    xpad = jnp.concatenate([st_ref[jj] for jj in range(CONV_W - 1)] + [xb], axis=0)
    xc = cb_ref[...]
    for jj in range(CONV_W):
        xc = xc + cw_ref[jj:jj + 1, :] * xpad[jj * bd:(jj + n_new) * bd]
    for jj in range(CONV_W - 1):
        newst_ref[jj] = xpad[(n_new + jj) * bd:(n_new + jj + 1) * bd]
    xc_bf = xc.astype(BF16)
    for j in range(N_GATE_TILES):
        lo = j * GATE_TILE
        w = min(GATE_TILE, D_RNN - lo)
        a, u = _rglru_gate_tile(j, xc, xc_bf, wband_ref, bg_ref, lam_ref)
        h = h0_ref[:, lo:lo + w]
        for t in range(n_new):
            h = a[t * bd:(t + 1) * bd] * h + u[t * bd:(t + 1) * bd]
            h_sc[t * bd:(t + 1) * bd, lo:lo + w] = h
        hlast_ref[:, lo:lo + w] = h
    mixed = (h_sc[...] * jax.nn.gelu(gate_br)).astype(BF16)
    y_ref[...] = _layer_norm(ALPHA * x_ref[...] + _dot(mixed, wout_ref[...]), g_ref[...], b_ref[...])


def rglru_sample(x_bf_tm, x_tm, state_tm, h0, w_in, conv_w, conv_b, wband, b_gate, lam, w_out, g, b):
    m, d = x_tm.shape
    bd = h0.shape[0]
    full = lambda shape: pl.BlockSpec(shape, lambda: (0,) * len(shape))
    args = (x_bf_tm, x_tm, state_tm, h0, w_in, conv_w, conv_b.reshape(1, D_RNN), wband, b_gate,
            lam.reshape(1, D_RNN), w_out, g.reshape(1, d), b.reshape(1, d))
    return pl.pallas_call(
        functools.partial(_rglru_sample_kernel, n_new=m // bd),
        in_specs=[full(a.shape) for a in args],
        out_specs=[full((m, d)), full((CONV_W - 1, bd, D_RNN)), full((bd, D_RNN))],
        out_shape=[jax.ShapeDtypeStruct((m, d), F32), jax.ShapeDtypeStruct((CONV_W - 1, bd, D_RNN), F32),
                   jax.ShapeDtypeStruct((bd, D_RNN), F32)],
        scratch_shapes=[pltpu.VMEM((m, D_RNN), F32)],
        compiler_params=pltpu.CompilerParams(vmem_limit_bytes=VMEM_LIMIT),
        name="rglru_sample",
    )(*args)


def _rope_table_kernel(cos_ref, sin_ref, *, pos0, period):
    n, half = cos_ref.shape
    r = lax.broadcasted_iota(jnp.int32, (n, half), 0)
    i = lax.broadcasted_iota(jnp.int32, (n, half), 1)
    pos = (pos0 + r % period).astype(F32)
    inv = jnp.exp(i.astype(F32) * (-math.log(ROPE_BASE) / half))
    ang = pos * inv
    cos_ref[...] = jnp.cos(ang)
    sin_ref[...] = jnp.sin(ang)


def rope_tables(n, pos0, period):
    half = DK_C // 2
    spec = pl.BlockSpec((n, half), lambda: (0, 0))
    return pl.pallas_call(
        functools.partial(_rope_table_kernel, pos0=pos0, period=period),
        out_specs=[spec, spec],
        out_shape=[jax.ShapeDtypeStruct((n, half), F32)] * 2,
        name="rope_tables",
    )()


def _ret_proj_kernel(x_ref, w_ref, cos_ref, sin_ref, q_ref, k_ref, v_ref, g_ref):
    x = x_ref[0]
    cos = cos_ref[...]
    sin = sin_ref[...]
    half = DK_C // 2
    qk_w = H_C * DK_C
    v_w = H_C * DV_C

    def roped(y, scale):
        outs = []
        for h in range(H_C):
            x1 = y[:, h * DK_C:h * DK_C + half]
            x2 = y[:, h * DK_C + half:(h + 1) * DK_C]
            outs += [x1 * cos - x2 * sin, x1 * sin + x2 * cos]
        return (jnp.concatenate(outs, axis=1) * scale).astype(BF16)

    q_ref[0] = roped(_dot(x, w_ref[:, 0:qk_w]), 1.0)
    k_ref[0] = roped(_dot(x, w_ref[:, qk_w:2 * qk_w]), DK_C ** -0.5)
    v_ref[0] = _dot(x, w_ref[:, 2 * qk_w:2 * qk_w + v_w]).astype(BF16)
    g_ref[0] = _dot(x, w_ref[:, 2 * qk_w + v_w:2 * qk_w + 2 * v_w])


def ret_proj(x_bf, w_in, cos, sin, tm):
    bsz, seq, d = x_bf.shape
    qk_w, v_w = H_C * DK_C, H_C * DV_C
    xmap = lambda b, i: (b, i, 0)
    tmap = lambda b, i: (i, 0)
    return pl.pallas_call(
        _ret_proj_kernel,
        grid=(bsz, seq // tm),
        in_specs=[pl.BlockSpec((1, tm, d), xmap), _const_spec((d, 2 * qk_w + 2 * v_w)),
                  pl.BlockSpec((tm, DK_C // 2), tmap), pl.BlockSpec((tm, DK_C // 2), tmap)],
        out_specs=[pl.BlockSpec((1, tm, qk_w), xmap), pl.BlockSpec((1, tm, qk_w), xmap),
                   pl.BlockSpec((1, tm, v_w), xmap), pl.BlockSpec((1, tm, v_w), xmap)],
        out_shape=[jax.ShapeDtypeStruct((bsz, seq, qk_w), BF16), jax.ShapeDtypeStruct((bsz, seq, qk_w), BF16),
                   jax.ShapeDtypeStruct((bsz, seq, v_w), BF16), jax.ShapeDtypeStruct((bsz, seq, v_w), F32)],
        compiler_params=_params("parallel", "parallel"),
        name="ret_proj",
    )(x_bf, w_in, cos, sin)


def _retention_kernel(lg_ref, q_ref, k_ref, v_ref, g_ref, gng_ref, gnb_ref, s0_ref, o_ref, sout_ref, s_sc,
                      *, t_true):
    c = pl.program_id(2)

    @pl.when(c == 0)
    def _load_state():
        s_sc[...] = s0_ref[...]

    lg = lg_ref[:, 0:1]
    q = q_ref[0]
    k = k_ref[0]
    v = v_ref[0]
    t = q.shape[0]
    ti = lax.broadcasted_iota(jnp.int32, (t, t), 0)
    si = lax.broadcasted_iota(jnp.int32, (t, t), 1)
    diff = (ti - si).astype(F32)
    decay = jnp.where(diff >= 0, jnp.exp(jnp.maximum(diff, 0.0) * lg), 0.0)
    scores = lax.dot_general(q, k, _NT, preferred_element_type=F32) * decay
    o = _dot(scores.astype(BF16), v)
    state = s_sc[...]
    tcol = lax.broadcasted_iota(jnp.int32, (t, 1), 0).astype(F32)
    o = o + _dot(q, state.astype(BF16)) * jnp.exp((tcol + 1.0) * lg)
    zeta = jnp.exp((t_true - 1.0 - tcol) * lg)
    kz = (k.astype(F32) * zeta).astype(BF16)
    new_state = jnp.exp(t_true * lg) * state + lax.dot_general(kz, v, _TN, preferred_element_type=F32)
    s_sc[...] = new_state

    mu = jnp.mean(o, axis=-1, keepdims=True)
    oc = o - mu
    var = jnp.mean(oc * oc, axis=-1, keepdims=True)
    on = oc * lax.rsqrt(var + GN_EPS) * gng_ref[...] + gnb_ref[...]
    o_ref[0] = (jax.nn.silu(g_ref[0]) * on).astype(BF16)

    @pl.when(c == pl.num_programs(2) - 1)
    def _store_state():
        sout_ref[...] = new_state


def retention(log_g, q, k, v, g, gn_g, gn_b, s0, chunk, t_true):
    bsz, seq, _ = q.shape
    v_w = H_C * DV_C
    qmap = lambda b, h, c: (b, c, h)
    smap = lambda b, h, c: (b, h, 0, 0)
    hmap = lambda b, h, c: (0, h)
    return pl.pallas_call(
        functools.partial(_retention_kernel, t_true=float(t_true)),
        grid=(bsz, H_C, seq // chunk),
        in_specs=[pl.BlockSpec((None, 1, LANES), lambda b, h, c: (h, 0, 0)),
                  pl.BlockSpec((1, chunk, DK_C), qmap), pl.BlockSpec((1, chunk, DK_C), qmap),
                  pl.BlockSpec((1, chunk, DV_C), qmap), pl.BlockSpec((1, chunk, DV_C), qmap),
                  pl.BlockSpec((1, DV_C), hmap), pl.BlockSpec((1, DV_C), hmap),
                  pl.BlockSpec((None, None, DK_C, DV_C), smap)],
        out_specs=[pl.BlockSpec((1, chunk, DV_C), qmap), pl.BlockSpec((None, None, DK_C, DV_C), smap)],
        out_shape=[jax.ShapeDtypeStruct((bsz, seq, v_w), BF16),
                   jax.ShapeDtypeStruct((bsz, H_C, DK_C, DV_C), F32)],
        scratch_shapes=[pltpu.VMEM((DK_C, DV_C), F32)],
        compiler_params=_params("parallel", "parallel", "arbitrary"),
        name="retention",
    )(log_g, q, k, v, g, gn_g.reshape(1, v_w), gn_b.reshape(1, v_w), s0)


TM_DENSE = 512
TQ_ATTN = 512
T_RGLRU = 512
PAGE_GROUP = 8
RET_CHUNK = 256
ATTN_HEAD_PAIRS = 4


def kernel(x_prompt, x_sample, cache_k_a, cache_v_a, cache_logf_a, page_table, state_conv_b, state_h_b, state_ret_c, w_in_a, b_f_a, w_out_a, w_in_b, conv_w_b, conv_b_b, w_gate_b, b_gate_b, lam_b, w_out_b, w_in_c, gn_g_c, gn_b_c, w_out_c, ln1_g, ln1_b, w_up_ffn, w_down_ffn, ln2_g, ln2_b):
    bsz, seq, d = x_prompt.shape
    bd, n_new, _ = x_sample.shape
    mp, ms = bsz * seq, bd * n_new
    past_len = page_table.shape[1] * cache_k_a.shape[2]

    xp = x_prompt.reshape(mp, d)
    xs = x_sample.reshape(ms, d)
    xp_bf = xp.astype(BF16)
    xs_bf = xs.astype(BF16)

    n_la, pool, page = cache_k_a.shape[:3]
    kT_pool = jnp.transpose(cache_k_a, (0, 1, 3, 4, 2)).reshape(n_la, pool, d, page)
    vT_pool = jnp.transpose(cache_v_a, (0, 1, 3, 4, 2)).reshape(n_la, pool, d, page)
    lfT_pool = jnp.transpose(cache_logf_a, (0, 1, 3, 2))

    log_g = jnp.log1p(-jnp.exp2(-5.0 - jnp.arange(H_C, dtype=F32)))
    log_g = jnp.broadcast_to(log_g[:, None, None], (H_C, 1, LANES))

    outs_a = {name: [] for name in ("lfp", "ks", "vs", "lfs")}
    out_b, out_c = {}, {}
    kv_all = None

    for i in range(DEPTH):
        jl, kind = divmod(i, 3)
        if kind == 0:
            w = w_in_a[jl]
            wq = w[:, 0:d].astype(BF16)
            wkT = w[:, d:2 * d].T.astype(BF16)
            wvT = w[:, 2 * d:3 * d].T.astype(BF16)
            wfT = w[:, 3 * d:].T.astype(BF16)
            w_out = w_out_a[jl].astype(BF16)
            q_bf, kT_all, vT_all, kT_bf, vT_bf, lfT = fox_proj(
                xp_bf.reshape(bsz, seq, d), wq, wkT, wvT, wfT, b_f_a[jl].reshape(H_A, 1), TM_DENSE,
                jl, n_la, kv_all)
            kv_all = (kT_all, vT_all)
            ct = fox_cumsum(lfT)
            o_bf = fox_attn(q_bf, kT_bf, vT_bf, jnp.transpose(ct, (0, 2, 1)), ct, TQ_ATTN, ATTN_HEAD_PAIRS)
            mix_p = (o_bf.reshape(mp, d), w_out)
            outs_a["lfp"].append(lfT)
            wf_pad = jnp.pad(w[:, 3 * d:], ((0, 0), (0, LANES - H_A))).astype(BF16)
            bf_pad = jnp.pad(b_f_a[jl], (0, LANES - H_A)).reshape(1, LANES)
            qs, ks, vs, lfs = fox_proj_sample(xs_bf, w[:, 0:3 * d].astype(BF16), wf_pad, bf_pad)
            lfs = lfs[:, :H_A].reshape(bd, n_new, H_A)
            lfs_T = jnp.pad(jnp.transpose(lfs, (0, 2, 1)), ((0, 0), (0, 0), (0, LANES - n_new)))
            o_s = fox_sample_attn(page_table, kT_pool, vT_pool, lfT_pool, jl, qs.reshape(bd, n_new, d),
                                  ks.reshape(bd, n_new, d), vs.reshape(bd, n_new, d), lfs_T, PAGE_GROUP)
            mix_s = (o_s.reshape(ms, d).astype(BF16), w_out)
            outs_a["ks"].append(ks.reshape(bd, n_new, H_A, DH_A))
            outs_a["vs"].append(vs.reshape(bd, n_new, H_A, DH_A))
            outs_a["lfs"].append(lfs)
        elif kind == 1:
            w_in = w_in_b[jl].astype(BF16)
            w_out = w_out_b[jl].astype(BF16)
            wband = _band_gate_weights(w_gate_b[jl])
            xp3, xp_bf3, tail, hlast = rglru_prompt(
                xp_bf.reshape(bsz, seq, d), xp.reshape(bsz, seq, d), w_in, conv_w_b[jl], conv_b_b[jl], wband,
                b_gate_b[jl], lam_b[jl], w_out, ln1_g[i], ln1_b[i], T_RGLRU)
            xp, xp_bf = xp3.reshape(mp, d), xp_bf3.reshape(mp, d)
            out_b["cp"] = tail[:, SUBLANES - (CONV_W - 1):, :]
            out_b["hp"] = hlast[:, 0, :]
            to_tm = lambda a: jnp.transpose(a.reshape(bd, n_new, d), (1, 0, 2)).reshape(ms, d)
            ys_tm, newst, hs = rglru_sample(
                to_tm(xs_bf), to_tm(xs), jnp.transpose(state_conv_b[jl], (1, 0, 2)), state_h_b[jl], w_in,
                conv_w_b[jl], conv_b_b[jl], wband, b_gate_b[jl], lam_b[jl], w_out, ln1_g[i], ln1_b[i])
            xs = jnp.transpose(ys_tm.reshape(n_new, bd, d), (1, 0, 2)).reshape(ms, d)
            xs_bf = xs.astype(BF16)
            out_b["cs"] = jnp.transpose(newst, (1, 0, 2))
            out_b["hs"] = hs
            mix_p = mix_s = None
        else:
            w_in = w_in_c[jl].astype(BF16)
            w_out = w_out_c[jl].astype(BF16)
            v_w = H_C * DV_C
            cos_p, sin_p = rope_tables(seq, 0, seq)
            q, k, v, g = ret_proj(xp_bf.reshape(bsz, seq, d), w_in, cos_p, sin_p, TM_DENSE)
            chunk = next((c for c in (RET_CHUNK, CHUNK_C) if seq % c == 0), seq)
            og, s_p = retention(log_g, q, k, v, g, gn_g_c[jl], gn_b_c[jl],
                                jnp.zeros((bsz, H_C, DK_C, DV_C), F32), chunk, chunk)
            mix_p = (og.reshape(mp, v_w), w_out)
            out_c["sp"] = s_p
            cos_s, sin_s = rope_tables(ms, past_len, n_new)
            q, k, v, g = ret_proj(xs_bf.reshape(1, ms, d), w_in, cos_s, sin_s, ms)
            pad_rows = lambda a: jnp.pad(a.reshape(bd, n_new, a.shape[-1]), ((0, 0), (0, SUBLANES - n_new), (0, 0)))
            og, s_s = retention(log_g, pad_rows(q), pad_rows(k), pad_rows(v), pad_rows(g), gn_g_c[jl], gn_b_c[jl],
                                state_ret_c[jl], SUBLANES, n_new)
            mix_s = (og[:, :n_new].reshape(ms, v_w), w_out)
            out_c["ss"] = s_s
        if mix_p is not None:
            xp, xp_bf = proj_ln(mix_p[0], mix_p[1], xp, ln1_g[i], ln1_b[i], TM_DENSE)
            xs, xs_bf = proj_ln(mix_s[0], mix_s[1], xs, ln1_g[i], ln1_b[i], TM_DENSE)
        w_up = w_up_ffn[i].astype(BF16)
        w_dn = w_down_ffn[i].astype(BF16)
        xp, xp_bf = ffn_ln(xp_bf, xp, w_up, w_dn, ln2_g[i], ln2_b[i], TM_DENSE)
        xs, xs_bf = ffn_ln(xs_bf, xs, w_up, w_dn, ln2_g[i], ln2_b[i], TM_DENSE)

    kv_out = lambda a: jnp.transpose(a.reshape(n_la, bsz, H_A, DH_A, seq), (0, 1, 4, 2, 3))
    return (xp.reshape(bsz, seq, d), xs.reshape(bd, n_new, d),
            kv_out(kv_all[0]), kv_out(kv_all[1]), jnp.transpose(jnp.stack(outs_a["lfp"]), (0, 1, 3, 2)),
            jnp.stack(outs_a["ks"]), jnp.stack(outs_a["vs"]), jnp.stack(outs_a["lfs"]),
            out_b["cp"][None], out_b["hp"][None], out_b["cs"][None], out_b["hs"][None],
            out_c["sp"][None], out_c["ss"][None])
```

```python
import functools
import math

import jax
import jax.numpy as jnp
from jax import lax
from jax.experimental import pallas as pl
from jax.experimental.pallas import tpu as pltpu

F32 = jnp.float32
BF16 = jnp.bfloat16

D_MODEL = 1024
DEPTH = 4
H_A = 16
DH_A = D_MODEL // H_A
D_RNN = 1408
N_BLK_B = 16
BW_B = D_RNN // N_BLK_B
CONV_W = 4
LRU_C = 8.0
H_C = 4
DK_C = D_MODEL // H_C
DV_C = 2 * DK_C
CHUNK_C = 128
ROPE_BASE = 10000.0
D_FF = 2816
ALPHA = (2 * DEPTH) ** 0.25
LN_EPS = 1e-5
GN_EPS = 1e-5
LOG2E = math.log2(math.e)

LANES = 128
SUBLANES = 8
MXU_COLS = 256
VMEM_LIMIT = 56 * 1024 * 1024

_NT = (((1,), (1,)), ((), ()))
_TN = (((0,), (0,)), ((), ()))


def _params(*sem):
    return pltpu.CompilerParams(dimension_semantics=sem, vmem_limit_bytes=VMEM_LIMIT)


def _const_spec(shape):
    zeros = (0,) * len(shape)
    return pl.BlockSpec(shape, lambda *_: zeros, pipeline_mode=pl.Buffered(1))


def _dot(a, b):
    return jnp.dot(a, b, preferred_element_type=F32)


def _layer_norm(z, g, b):
    mu = jnp.mean(z, axis=-1, keepdims=True)
    zc = z - mu
    var = jnp.mean(zc * zc, axis=-1, keepdims=True)
    return zc * lax.rsqrt(var + LN_EPS) * g + b


def _log_sigmoid(z):
    return jnp.minimum(z, 0.0) - jnp.log1p(jnp.exp(-jnp.abs(z)))


def _split3(x):
    hi = x.astype(BF16)
    r1 = x - hi.astype(F32)
    mid = r1.astype(BF16)
    lo = (r1 - mid.astype(F32)).astype(BF16)
    return hi, mid, lo


def _dot3(pieces, m01):
    hi, mid, lo = pieces
    return _dot(hi, m01) + _dot(mid, m01) + _dot(lo, m01)


def _dot_f32_by_01(x, m01):
    return _dot3(_split3(x), m01)


def _tri01(n, strict_lower_rows):
    r = lax.broadcasted_iota(jnp.int32, (n, n), 0)
    c = lax.broadcasted_iota(jnp.int32, (n, n), 1)
    keep = (r > c) if strict_lower_rows else (r <= c)
    return jnp.where(keep, 1.0, 0.0).astype(BF16)


def _proj_ln_kernel(a_ref, w_ref, x_ref, g_ref, b_ref, y_ref, ybf_ref):
    m = _dot(a_ref[...], w_ref[...])
    y = _layer_norm(ALPHA * x_ref[...] + m, g_ref[...], b_ref[...])
    y_ref[...] = y
    ybf_ref[...] = y.astype(BF16)


def proj_ln(a_bf, w_bf, x, g, b, tm):
    m, k = a_bf.shape
    d = w_bf.shape[1]
    tm = min(tm, m)
    row = lambda i: (i, 0)
    return pl.pallas_call(
        _proj_ln_kernel,
        grid=(m // tm,),
        in_specs=[pl.BlockSpec((tm, k), row), _const_spec((k, d)), pl.BlockSpec((tm, d), row),
                  _const_spec((1, d)), _const_spec((1, d))],
        out_specs=[pl.BlockSpec((tm, d), row), pl.BlockSpec((tm, d), row)],
        out_shape=[jax.ShapeDtypeStruct((m, d), F32), jax.ShapeDtypeStruct((m, d), BF16)],
        compiler_params=_params("parallel"),
        name="proj_ln",
    )(a_bf, w_bf, x, g.reshape(1, d), b.reshape(1, d))


def _ffn_ln_kernel(xbf_ref, x_ref, wup_ref, wdn_ref, g_ref, b_ref, y_ref, ybf_ref, *, n_chunks):
    xb = xbf_ref[...]
    cw = D_FF // n_chunks
    acc = None
    for c in range(n_chunks):
        gate = _dot(xb, wup_ref[:, c * cw:(c + 1) * cw])
        up = _dot(xb, wup_ref[:, D_FF + c * cw:D_FF + (c + 1) * cw])
        h = (jax.nn.silu(gate) * up).astype(BF16)
        part = _dot(h, wdn_ref[c * cw:(c + 1) * cw, :])
        acc = part if acc is None else acc + part
    y = _layer_norm(ALPHA * x_ref[...] + acc, g_ref[...], b_ref[...])
    y_ref[...] = y
    ybf_ref[...] = y.astype(BF16)


def ffn_ln(x_bf, x, w_up_bf, w_dn_bf, g, b, tm, n_chunks=1):
    m, d = x.shape
    tm = min(tm, m)
    row = lambda i: (i, 0)
    return pl.pallas_call(
        functools.partial(_ffn_ln_kernel, n_chunks=n_chunks),
        grid=(m // tm,),
        in_specs=[pl.BlockSpec((tm, d), row), pl.BlockSpec((tm, d), row),
                  _const_spec((d, 2 * D_FF)), _const_spec((D_FF, d)),
                  _const_spec((1, d)), _const_spec((1, d))],
        out_specs=[pl.BlockSpec((tm, d), row), pl.BlockSpec((tm, d), row)],
        out_shape=[jax.ShapeDtypeStruct((m, d), F32), jax.ShapeDtypeStruct((m, d), BF16)],
        compiler_params=_params("parallel"),
        name="ffn_ln",
    )(x_bf, x, w_up_bf, w_dn_bf, g.reshape(1, d), b.reshape(1, d))


def _fox_proj_kernel(*refs, layer, n_layers, first):
    if first:
        x_ref, wq_ref, wkT_ref, wvT_ref, wfT_ref, bf_ref = refs[:6]
    else:
        x_ref, wq_ref, wkT_ref, wvT_ref, wfT_ref, bf_ref, _, _ = refs[:8]
    q_ref, kT_ref, vT_ref, kTb_ref, vTb_ref, lfT_ref = refs[-6:]
    x = x_ref[0].astype(BF16)
    q_ref[0] = (_dot(x, wq_ref[...]) * (DH_A ** -0.5 * LOG2E)).astype(BF16)
    kT = lax.dot_general(wkT_ref[...], x, _NT, preferred_element_type=F32)
    vT = lax.dot_general(wvT_ref[...], x, _NT, preferred_element_type=F32)
    if first:
        for l in range(n_layers):
            kT_ref[l, 0] = kT if l == layer else jnp.zeros_like(kT)
            vT_ref[l, 0] = vT if l == layer else jnp.zeros_like(vT)
    else:
        kT_ref[0] = kT
        vT_ref[0] = vT
    kTb_ref[0] = kT.astype(BF16)
    vTb_ref[0] = vT.astype(BF16)
    f = lax.dot_general(wfT_ref[...], x, _NT, preferred_element_type=F32)
    lfT_ref[0] = _log_sigmoid(f + bf_ref[...])


def fox_proj(x_bf, wq, wkT, wvT, wfT, bf_col, tm, layer, n_layers, kv_all=None):
    bsz, seq, d = x_bf.shape
    first = kv_all is None
    xmap = lambda b, i: (b, i, 0)
    tmap = lambda b, i: (b, 0, i)
    if first:
        kv_spec = pl.BlockSpec((n_layers, 1, d, tm), lambda b, i: (0, b, 0, i))
        extra_in, extra_specs, aliases = (), [], {}
    else:
        kv_spec = pl.BlockSpec((None, 1, d, tm), lambda b, i: (layer, b, 0, i))
        extra_in = tuple(kv_all)
        extra_specs = [pl.BlockSpec(memory_space=pl.ANY)] * 2
        aliases = {6: 1, 7: 2}
    kv_shape = jax.ShapeDtypeStruct((n_layers, bsz, d, seq), F32)
    return pl.pallas_call(
        functools.partial(_fox_proj_kernel, layer=layer, n_layers=n_layers, first=first),
        grid=(bsz, seq // tm),
        in_specs=[pl.BlockSpec((1, tm, d), xmap), _const_spec((d, d)), _const_spec((d, d)),
                  _const_spec((d, d)), _const_spec((H_A, d)), _const_spec((H_A, 1))] + extra_specs,
        out_specs=[pl.BlockSpec((1, tm, d), xmap), kv_spec, kv_spec, pl.BlockSpec((1, d, tm), tmap),
                   pl.BlockSpec((1, d, tm), tmap), pl.BlockSpec((1, H_A, tm), tmap)],
        out_shape=[jax.ShapeDtypeStruct((bsz, seq, d), BF16), kv_shape, kv_shape,
                   jax.ShapeDtypeStruct((bsz, d, seq), BF16), jax.ShapeDtypeStruct((bsz, d, seq), BF16),
                   jax.ShapeDtypeStruct((bsz, H_A, seq), F32)],
        input_output_aliases=aliases,
        compiler_params=_params("parallel", "parallel"),
        name="fox_proj",
    )(x_bf, wq, wkT, wvT, wfT, bf_col, *extra_in)


def _fox_cumsum_kernel(lf_ref, ct_ref, *, chunk):
    seq = lf_ref.shape[2]
    prefix = _tri01(chunk, strict_lower_rows=False)
    carry = jnp.zeros((H_A, 1), F32)
    for i in range(seq // chunk):
        cs = _dot_f32_by_01(lf_ref[0, :, i * chunk:(i + 1) * chunk], prefix) + carry
        ct_ref[0, :, i * chunk:(i + 1) * chunk] = cs
        carry = cs[:, chunk - 1:chunk]


def fox_cumsum(lfT):
    bsz, h, seq = lfT.shape
    chunk = min(seq, MXU_COLS)
    spec = pl.BlockSpec((1, h, seq), lambda b: (b, 0, 0))
    return pl.pallas_call(
        functools.partial(_fox_cumsum_kernel, chunk=chunk),
        grid=(bsz,), in_specs=[spec], out_specs=spec,
        out_shape=jax.ShapeDtypeStruct((bsz, h, seq), F32),
        compiler_params=_params("parallel"),
        name="fox_cumsum",
    )(lfT)


def _fox_attn_kernel(qi_ref, kj_ref, q_ref, kT_ref, vT_ref, c_ref, ct_ref, o_ref, cq_sc, m_sc, l_sc, acc_sc,
                     *, tq, n_pairs):
    g = pl.program_id(1)
    i = qi_ref[pl.program_id(2)]
    j = kj_ref[pl.program_id(2)]
    tk = kT_ref.shape[2]

    def spread(x, width):
        return jnp.concatenate([x] * (width // LANES), axis=1) if width >= LANES else x[:, 0:width]

    @pl.when(j == 0)
    def _init():
        cblk = c_ref[0]
        hl = lax.broadcasted_iota(jnp.int32, cblk.shape, 1)
        for hh in range(2 * n_pairs):
            cq = jnp.sum(jnp.where(hl == 2 * n_pairs * g + hh, cblk, 0.0), axis=1, keepdims=True)
            cq_sc[hh] = jnp.broadcast_to(cq * LOG2E, (tq, LANES))
        m_sc[...] = jnp.full(m_sc.shape, -jnp.inf, F32)
        l_sc[...] = jnp.zeros(l_sc.shape, F32)
        acc_sc[...] = jnp.zeros(acc_sc.shape, F32)

    def step(row_lo, n_rows, n_keys, masked):
        rows = slice(row_lo, row_lo + n_rows)
        first_head = lax.broadcasted_iota(jnp.int32, (n_rows, LANES), 1) < DH_A
        for pp in range(n_pairs):
            q = q_ref[0, rows, pp * LANES:(pp + 1) * LANES]
            kT = kT_ref[0, pp * LANES:(pp + 1) * LANES, 0:n_keys]
            vT = vT_ref[0, pp * LANES:(pp + 1) * LANES, 0:n_keys]
            alphas, pvs = [], []
            for a in range(2):
                hh = 2 * pp + a
                qa = jnp.where(first_head if a == 0 else jnp.logical_not(first_head), q, jnp.zeros_like(q))
                ck = ct_ref[0, pl.ds(2 * n_pairs * g + hh, 1), 0:n_keys] * LOG2E
                y = _dot(qa, kT) - ck
                if masked:
                    r = lax.broadcasted_iota(jnp.int32, y.shape, 0) + row_lo
                    c = lax.broadcasted_iota(jnp.int32, y.shape, 1)
                    y = jnp.where(r >= c, y, -jnp.inf)
                cq = cq_sc[hh, rows]
                m_old = m_sc[hh, rows]
                m_new = jnp.maximum(m_old, jnp.max(y, axis=1, keepdims=True) + cq)
                alpha = jnp.exp2(m_old - m_new)
                pr = jnp.exp2(y - spread(m_new - cq, n_keys))
                l_sc[hh, rows] = alpha * l_sc[hh, rows] + jnp.sum(pr, axis=1, keepdims=True)
                m_sc[hh, rows] = m_new
                pvs.append(lax.dot_general(pr.astype(BF16), vT, _NT, preferred_element_type=F32))
                alphas.append(alpha)
            acc_sc[pp, rows] = (acc_sc[pp, rows] * jnp.where(first_head, alphas[0], alphas[1])
                                + jnp.where(first_head, pvs[0], pvs[1]))

    @pl.when(j < i)
    def _below():
        step(0, tq, tk, False)

    @pl.when(j == i)
    def _diag():
        step(0, tq // 2, tk // 2, True)
        step(tq // 2, tq // 2, tk, True)
        first_head = lax.broadcasted_iota(jnp.int32, (tq, LANES), 1) < DH_A
        for pp in range(n_pairs):
            norm = jnp.where(first_head, l_sc[2 * pp], l_sc[2 * pp + 1])
            o_ref[0, :, pp * LANES:(pp + 1) * LANES] = (acc_sc[pp] / norm).astype(BF16)


def fox_attn(q_bf, kT_bf, vT_bf, c, ct, tq, n_pairs):
    bsz, seq, d = q_bf.shape
    nq = seq // tq
    wid = n_pairs * LANES
    pairs = [(i, j) for i in range(nq) for j in range(i + 1)]
    qi = jnp.asarray([ij[0] for ij in pairs], jnp.int32)
    kj = jnp.asarray([ij[1] for ij in pairs], jnp.int32)
    qmap = lambda b, g, s, qi, kj: (b, qi[s], g)
    kmap = lambda b, g, s, qi, kj: (b, g, kj[s])
    grid_spec = pltpu.PrefetchScalarGridSpec(
        num_scalar_prefetch=2,
        grid=(bsz, d // wid, len(pairs)),
        in_specs=[pl.BlockSpec((1, tq, wid), qmap), pl.BlockSpec((1, wid, tq), kmap),
                  pl.BlockSpec((1, wid, tq), kmap),
                  pl.BlockSpec((1, tq, H_A), lambda b, g, s, qi, kj: (b, qi[s], 0)),
                  pl.BlockSpec((1, H_A, tq), lambda b, g, s, qi, kj: (b, 0, kj[s]))],
        out_specs=pl.BlockSpec((1, tq, wid), qmap),
        scratch_shapes=[pltpu.VMEM((2 * n_pairs, tq, LANES), F32), pltpu.VMEM((2 * n_pairs, tq, LANES), F32),
                        pltpu.VMEM((2 * n_pairs, tq, LANES), F32), pltpu.VMEM((n_pairs, tq, LANES), F32)],
    )
    return pl.pallas_call(
        functools.partial(_fox_attn_kernel, tq=tq, n_pairs=n_pairs),
        grid_spec=grid_spec,
        out_shape=jax.ShapeDtypeStruct((bsz, seq, d), BF16),
        compiler_params=_params("parallel", "parallel", "arbitrary"),
        name="fox_attn",
    )(qi, kj, q_bf, kT_bf, vT_bf, c, ct)


def _fox_proj_sample_kernel(x_ref, w_ref, wf_ref, bf_ref, q_ref, k_ref, v_ref, lf_ref):
    x = x_ref[...]
    d = x.shape[1]
    q_ref[...] = _dot(x, w_ref[:, 0:d])
    k_ref[...] = _dot(x, w_ref[:, d:2 * d])
    v_ref[...] = _dot(x, w_ref[:, 2 * d:3 * d])
    lf_ref[...] = _log_sigmoid(_dot(x, wf_ref[...]) + bf_ref[...])


def fox_proj_sample(x_bf, w_qkv, wf_pad, bf_pad):
    m, d = x_bf.shape
    full = lambda shape: pl.BlockSpec(shape, lambda: (0,) * len(shape))
    return pl.pallas_call(
        _fox_proj_sample_kernel,
        in_specs=[full((m, d)), full((d, 3 * d)), full((d, LANES)), full((1, LANES))],
        out_specs=[full((m, d)), full((m, d)), full((m, d)), full((m, LANES))],
        out_shape=[jax.ShapeDtypeStruct((m, d), F32)] * 3 + [jax.ShapeDtypeStruct((m, LANES), F32)],
        compiler_params=pltpu.CompilerParams(vmem_limit_bytes=VMEM_LIMIT),
        name="fox_proj_sample",
    )(x_bf, w_qkv, wf_pad, bf_pad)


def _fox_sample_attn_kernel(pt_ref, *refs, n_new, group):
    kT_refs, vT_refs, lf_refs = refs[0:group], refs[group:2 * group], refs[2 * group:3 * group]
    (q_ref, kn_ref, vn_ref, lfs_ref, o_ref,
     qbd_sc, ctn_sc, ctcol_sc, m_sc, l_sc, acc_sc, carry_sc) = refs[3 * group:]
    j = pl.program_id(1)
    rows = n_new * H_A
    d = qbd_sc.shape[1]
    page = kT_refs[0].shape[1]

    def head_diag():
        r = lax.broadcasted_iota(jnp.int32, (rows, d), 0)
        c = lax.broadcasted_iota(jnp.int32, (rows, d), 1)
        return (c // DH_A) == (r % H_A)

    def spread(x, width):
        return jnp.concatenate([x] * (width // LANES), axis=1) if width >= LANES else x[:, 0:width]

    def online_update(s, pv_fn):
        m_old = m_sc[...]
        m_new = jnp.maximum(m_old, jnp.max(s, axis=1, keepdims=True))
        alpha = jnp.exp(m_old - m_new)
        pr = jnp.exp(s - spread(m_new, s.shape[1]))
        l_sc[...] = alpha * l_sc[...] + jnp.sum(pr, axis=1, keepdims=True)
        m_sc[...] = m_new
        acc_sc[...] = spread(alpha, d) * acc_sc[...] + pv_fn(pr.astype(BF16))

    @pl.when(j == 0)
    def _init():
        q = q_ref[0]
        qe = jnp.concatenate([jnp.broadcast_to(q[t:t + 1, :], (H_A, d)) for t in range(n_new)], axis=0)
        qbd_sc[...] = jnp.where(head_diag(), qe * DH_A ** -0.5, 0.0).astype(BF16)
        ctn = _dot_f32_by_01(lfs_ref[0], _tri01(LANES, strict_lower_rows=False))
        ctn_sc[...] = ctn
        ctcol = jnp.concatenate([ctn[:, t:t + 1] for t in range(n_new)], axis=0)
        ctcol_sc[...] = jnp.broadcast_to(ctcol, ctcol_sc.shape)
        m_sc[...] = jnp.full(m_sc.shape, -jnp.inf, F32)
        l_sc[...] = jnp.zeros(l_sc.shape, F32)
        acc_sc[...] = jnp.zeros(acc_sc.shape, F32)
        carry_sc[...] = jnp.zeros(carry_sc.shape, F32)

    qbd = qbd_sc[...]
    lf_all = jnp.concatenate([lf_refs[g][...] for g in range(group)], axis=0)
    suffix_all = _dot_f32_by_01(lf_all, _tri01(page, strict_lower_rows=True))
    carry = carry_sc[...]
    parts = []
    for g in range(group):
        bias = suffix_all[g * H_A:(g + 1) * H_A] + carry
        carry = carry + jnp.sum(lf_refs[g][...], axis=1, keepdims=True)
        parts.append(_dot(qbd, kT_refs[g][...].astype(BF16)) + jnp.concatenate([bias] * n_new, axis=0))
    carry_sc[...] = carry
    s = jnp.concatenate(parts, axis=1) + spread(ctcol_sc[...], group * page)

    def pv_pages(p):
        out = None
        for g in range(group):
            t = lax.dot_general(p[:, g * page:(g + 1) * page], vT_refs[g][...].astype(BF16), _NT,
                                preferred_element_type=F32)
            out = t if out is None else out + t
        return out

    online_update(s, pv_pages)

    @pl.when(j == pl.num_programs(1) - 1)
    def _finish():
        pad = jnp.zeros((SUBLANES - n_new, d), F32)
        kn = jnp.concatenate([kn_ref[0], pad], axis=0).astype(BF16)
        vn = jnp.concatenate([vn_ref[0], pad], axis=0).astype(BF16)
        sn = lax.dot_general(qbd_sc[...], kn, _NT, preferred_element_type=F32)
        ctk = jnp.concatenate([ctn_sc[:, 0:SUBLANES]] * n_new, axis=0)
        tq = lax.broadcasted_iota(jnp.int32, sn.shape, 0) // H_A
        tk = lax.broadcasted_iota(jnp.int32, sn.shape, 1)
        sn = jnp.where(tk <= tq, sn + (ctcol_sc[:, 0:SUBLANES] - ctk), -jnp.inf)
        online_update(sn, lambda p: _dot(p, vn))
        om = jnp.where(head_diag(), acc_sc[...] / spread(l_sc[...], d), 0.0)
        o_ref[0] = jnp.concatenate(
            [jnp.sum(om[t * H_A:(t + 1) * H_A], axis=0, keepdims=True) for t in range(n_new)], axis=0)


def fox_sample_attn(page_table, kT_pool, vT_pool, lfT_pool, layer, q, k_new, v_new, lfs_T, group):
    bd, n_new, d = q.shape
    n_pages = page_table.shape[1]
    page = kT_pool.shape[3]
    rows = n_new * H_A
    group = math.gcd(group, n_pages)
    pmaps = [lambda b, j, pt, g=g: (layer, pt[b, n_pages - 1 - (j * group + g)], 0, 0) for g in range(group)]
    bmap = lambda b, j, pt: (b, 0, 0)
    grid_spec = pltpu.PrefetchScalarGridSpec(
        num_scalar_prefetch=1,
        grid=(bd, n_pages // group),
        in_specs=([pl.BlockSpec((None, None, d, page), pm) for pm in pmaps]
                  + [pl.BlockSpec((None, None, d, page), pm) for pm in pmaps]
                  + [pl.BlockSpec((None, None, H_A, page), pm) for pm in pmaps]
                  + [pl.BlockSpec((1, n_new, d), bmap), pl.BlockSpec((1, n_new, d), bmap),
                     pl.BlockSpec((1, n_new, d), bmap), pl.BlockSpec((1, H_A, LANES), bmap)]),
        out_specs=pl.BlockSpec((1, n_new, d), bmap),
        scratch_shapes=[pltpu.VMEM((rows, d), BF16), pltpu.VMEM((H_A, LANES), F32),
                        pltpu.VMEM((rows, LANES), F32), pltpu.VMEM((rows, LANES), F32),
                        pltpu.VMEM((rows, LANES), F32), pltpu.VMEM((rows, d), F32), pltpu.VMEM((H_A, LANES), F32)],
    )
    return pl.pallas_call(
        functools.partial(_fox_sample_attn_kernel, n_new=n_new, group=group),
        grid_spec=grid_spec,
        out_shape=jax.ShapeDtypeStruct((bd, n_new, d), F32),
        compiler_params=_params("parallel", "arbitrary"),
        name="fox_sample_attn",
    )(page_table, *([kT_pool] * group), *([vT_pool] * group), *([lfT_pool] * group), q, k_new, v_new, lfs_T)


GATE_TILE = MXU_COLS
GATE_WIN = 2 * MXU_COLS
N_GATE_TILES = -(-D_RNN // GATE_TILE)


def _gate_window_start(j):
    first_block = (j * GATE_TILE) // BW_B
    return min((first_block * BW_B) // LANES * LANES, D_RNN - GATE_WIN)


def _band_gate_weights(w_gate):
    eye = jnp.eye(N_BLK_B, dtype=w_gate.dtype)
    tiles = []
    for part in range(2):
        wp = w_gate[:, :, part * BW_B:(part + 1) * BW_B]
        dense = (eye[:, None, :, None] * wp[:, :, None, :]).reshape(D_RNN, D_RNN)
        dense = jnp.pad(dense, ((0, 0), (0, N_GATE_TILES * GATE_TILE - D_RNN)))
        tiles.append(jnp.stack([
            dense[_gate_window_start(j):_gate_window_start(j) + GATE_WIN, j * GATE_TILE:(j + 1) * GATE_TILE]
            for j in range(N_GATE_TILES)]))
    return jnp.stack(tiles).astype(BF16)


def _rglru_gate_tile(j, xc, xc_bf, wband_ref, bg_ref, lam_ref):
    lo = j * GATE_TILE
    w = min(GATE_TILE, D_RNN - lo)
    ks = _gate_window_start(j)
    xw = xc_bf[:, ks:ks + GATE_WIN]
    r = jax.nn.sigmoid(_dot(xw, wband_ref[0, j])[:, :w] + bg_ref[0:1, lo:lo + w])
    ig = jax.nn.sigmoid(_dot(xw, wband_ref[1, j])[:, :w] + bg_ref[1:2, lo:lo + w])
    log_a = LRU_C * r * _log_sigmoid(lam_ref[:, lo:lo + w])
    a = jnp.exp(log_a)
    u = jnp.sqrt(-jnp.tanh(log_a) * (a * a + 1.0)) * (ig * xc[:, lo:lo + w])
    return a, u


def _rglru_prompt_kernel(xbf_ref, x_ref, win_ref, cw_ref, cb_ref, wband_ref, bg_ref, lam_ref, wout_ref,
                         g_ref, b_ref, y_ref, ybf_ref, tail_ref, hlast_ref, tail_sc, h_sc, a_sc, u_sc):
    t_rows = xbf_ref.shape[1]

    @pl.when(pl.program_id(1) == 0)
    def _reset():
        tail_sc[...] = jnp.zeros(tail_sc.shape, F32)
        h_sc[...] = jnp.zeros(h_sc.shape, F32)

    xb16 = xbf_ref[0]
    gate_br = _dot(xb16, win_ref[:, 0:D_RNN])
    xb = _dot(xb16, win_ref[:, D_RNN:2 * D_RNN])

    row8 = lax.broadcasted_iota(jnp.int32, (SUBLANES, D_RNN), 0)
    tail = tail_sc[...]
    xc = cb_ref[...] + cw_ref[CONV_W - 1:CONV_W, :] * xb
    for k in range(1, CONV_W):
        xs = pltpu.roll(xb, k, axis=0)
        head = jnp.where(row8 < k, pltpu.roll(tail, k, axis=0), xs[0:SUBLANES])
        xs = jnp.concatenate([head, xs[SUBLANES:]], axis=0)
        xc = xc + cw_ref[CONV_W - 1 - k:CONV_W - k, :] * xs
    new_tail = xb[t_rows - SUBLANES:t_rows]
    tail_sc[...] = new_tail
    tail_ref[0] = new_tail
    xc_bf = xc.astype(BF16)

    for j in range(N_GATE_TILES):
        lo = j * GATE_TILE
        w = min(GATE_TILE, D_RNN - lo)
        a, u = _rglru_gate_tile(j, xc, xc_bf, wband_ref, bg_ref, lam_ref)
        groups = (t_rows // SUBLANES, SUBLANES, w)
        a = a.reshape(groups)
        u = u.reshape(groups)
        pos = lax.broadcasted_iota(jnp.int32, groups, 1)
        for s in (1, 2, 4):
            live = pos >= s
            u = jnp.where(live, a * pltpu.roll(u, s, axis=1) + u, u)
            a = jnp.where(live, a * pltpu.roll(a, s, axis=1), a)
        a_sc[:, lo:lo + w] = a.reshape(t_rows, w)
        u_sc[:, lo:lo + w] = u.reshape(t_rows, w)

        def group(i, h_prev, lo=lo, w=w):
            blk = pl.ds(pl.multiple_of(i * SUBLANES, SUBLANES), SUBLANES)
            hb = a_sc[blk, lo:lo + w] * h_prev + u_sc[blk, lo:lo + w]
            u_sc[blk, lo:lo + w] = hb
            return hb[SUBLANES - 1:SUBLANES, :]

        h_sc[:, lo:lo + w] = lax.fori_loop(0, t_rows // SUBLANES, group, h_sc[:, lo:lo + w])

    hlast_ref[0] = jnp.broadcast_to(h_sc[...], (SUBLANES, D_RNN))
    mixed = (u_sc[...] * jax.nn.gelu(gate_br)).astype(BF16)
    y = _layer_norm(ALPHA * x_ref[0] + _dot(mixed, wout_ref[...]), g_ref[...], b_ref[...])
    y_ref[0] = y
    ybf_ref[0] = y.astype(BF16)


def rglru_prompt(x_bf, x, w_in, conv_w, conv_b, wband, b_gate, lam, w_out, g, b, t_rows):
    bsz, seq, d = x.shape
    xmap = lambda bi, ti: (bi, ti, 0)
    smap = lambda bi, ti: (bi, 0, 0)
    return pl.pallas_call(
        _rglru_prompt_kernel,
        grid=(bsz, seq // t_rows),
        in_specs=[pl.BlockSpec((1, t_rows, d), xmap), pl.BlockSpec((1, t_rows, d), xmap),
                  _const_spec((d, 2 * D_RNN)), _const_spec((CONV_W, D_RNN)), _const_spec((1, D_RNN)),
                  _const_spec((2, N_GATE_TILES, GATE_WIN, GATE_TILE)), _const_spec((2, D_RNN)),
                  _const_spec((1, D_RNN)), _const_spec((D_RNN, d)), _const_spec((1, d)), _const_spec((1, d))],
        out_specs=[pl.BlockSpec((1, t_rows, d), xmap), pl.BlockSpec((1, t_rows, d), xmap),
                   pl.BlockSpec((1, SUBLANES, D_RNN), smap), pl.BlockSpec((1, SUBLANES, D_RNN), smap)],
        out_shape=[jax.ShapeDtypeStruct((bsz, seq, d), F32), jax.ShapeDtypeStruct((bsz, seq, d), BF16),
                   jax.ShapeDtypeStruct((bsz, SUBLANES, D_RNN), F32),
                   jax.ShapeDtypeStruct((bsz, SUBLANES, D_RNN), F32)],
        scratch_shapes=[pltpu.VMEM((SUBLANES, D_RNN), F32), pltpu.VMEM((1, D_RNN), F32),
                        pltpu.VMEM((t_rows, D_RNN), F32), pltpu.VMEM((t_rows, D_RNN), F32)],
        compiler_params=_params("parallel", "arbitrary"),
        name="rglru_prompt",
    )(x_bf, x, w_in, conv_w, conv_b.reshape(1, D_RNN), wband, b_gate, lam.reshape(1, D_RNN), w_out,
      g.reshape(1, d), b.reshape(1, d))


def _rglru_sample_kernel(xbf_ref, x_ref, st_ref, h0_ref, win_ref, cw_ref, cb_ref, wband_ref, bg_ref, lam_ref,
                         wout_ref, g_ref, b_ref, y_ref, newst_ref, hlast_ref, h_sc, *, n_new):
    bd = h0_ref.shape[0]
    xb16 = xbf_ref[...]
    gate_br = _dot(xb16, win_ref[:, 0:D_RNN])
    xb = _dot(xb16, win_ref[:, D_RNN:2 * D_RNN])
    xpad = jnp.concatenate([st_ref[jj] for jj in range(CONV_W - 1)] + [xb], axis=0)
    xc = cb_ref[...]
    for jj in range(CONV_W):
        xc = xc + cw_ref[jj:jj + 1, :] * xpad[jj * bd:(jj + n_new) * bd]
    for jj in range(CONV_W - 1):
        newst_ref[jj] = xpad[(n_new + jj) * bd:(n_new + jj + 1) * bd]
    xc_bf = xc.astype(BF16)
    for j in range(N_GATE_TILES):
        lo = j * GATE_TILE
        w = min(GATE_TILE, D_RNN - lo)
        a, u = _rglru_gate_tile(j, xc, xc_bf, wband_ref, bg_ref, lam_ref)
        h = h0_ref[:, lo:lo + w]
        for t in range(n_new):
            h = a[t * bd:(t + 1) * bd] * h + u[t * bd:(t + 1) * bd]
            h_sc[t * bd:(t + 1) * bd, lo:lo + w] = h
        hlast_ref[:, lo:lo + w] = h
    mixed = (h_sc[...] * jax.nn.gelu(gate_br)).astype(BF16)
    y_ref[...] = _layer_norm(ALPHA * x_ref[...] + _dot(mixed, wout_ref[...]), g_ref[...], b_ref[...])


def rglru_sample(x_bf_tm, x_tm, state_tm, h0, w_in, conv_w, conv_b, wband, b_gate, lam, w_out, g, b):
    m, d = x_tm.shape
    bd = h0.shape[0]
    full = lambda shape: pl.BlockSpec(shape, lambda: (0,) * len(shape))
    args = (x_bf_tm, x_tm, state_tm, h0, w_in, conv_w, conv_b.reshape(1, D_RNN), wband, b_gate,
            lam.reshape(1, D_RNN), w_out, g.reshape(1, d), b.reshape(1, d))
    return pl.pallas_call(
        functools.partial(_rglru_sample_kernel, n_new=m // bd),
        in_specs=[full(a.shape) for a in args],
        out_specs=[full((m, d)), full((CONV_W - 1, bd, D_RNN)), full((bd, D_RNN))],
        out_shape=[jax.ShapeDtypeStruct((m, d), F32), jax.ShapeDtypeStruct((CONV_W - 1, bd, D_RNN), F32),
                   jax.ShapeDtypeStruct((bd, D_RNN), F32)],
        scratch_shapes=[pltpu.VMEM((m, D_RNN), F32)],
        compiler_params=pltpu.CompilerParams(vmem_limit_bytes=VMEM_LIMIT),
        name="rglru_sample",
    )(*args)


def _rope_table_kernel(cos_ref, sin_ref, *, pos0, period):
    n, half = cos_ref.shape
    r = lax.broadcasted_iota(jnp.int32, (n, half), 0)
    i = lax.broadcasted_iota(jnp.int32, (n, half), 1)
    pos = (pos0 + r % period).astype(F32)
    inv = jnp.exp(i.astype(F32) * (-math.log(ROPE_BASE) / half))
    ang = pos * inv
    cos_ref[...] = jnp.cos(ang)
    sin_ref[...] = jnp.sin(ang)


def rope_tables(n, pos0, period):
    half = DK_C // 2
    spec = pl.BlockSpec((n, half), lambda: (0, 0))
    return pl.pallas_call(
        functools.partial(_rope_table_kernel, pos0=pos0, period=period),
        out_specs=[spec, spec],
        out_shape=[jax.ShapeDtypeStruct((n, half), F32)] * 2,
        name="rope_tables",
    )()


def _ret_proj_kernel(x_ref, w_ref, cos_ref, sin_ref, q_ref, k_ref, v_ref, g_ref):
    x = x_ref[0]
    cos = cos_ref[...]
    sin = sin_ref[...]
    half = DK_C // 2
    qk_w = H_C * DK_C
    v_w = H_C * DV_C

    def roped(y, scale):
        outs = []
        for h in range(H_C):
            x1 = y[:, h * DK_C:h * DK_C + half]
            x2 = y[:, h * DK_C + half:(h + 1) * DK_C]
            outs += [x1 * cos - x2 * sin, x1 * sin + x2 * cos]
        return (jnp.concatenate(outs, axis=1) * scale).astype(BF16)

    q_ref[0] = roped(_dot(x, w_ref[:, 0:qk_w]), 1.0)
    k_ref[0] = roped(_dot(x, w_ref[:, qk_w:2 * qk_w]), DK_C ** -0.5)
    v_ref[0] = _dot(x, w_ref[:, 2 * qk_w:2 * qk_w + v_w]).astype(BF16)
    g_ref[0] = _dot(x, w_ref[:, 2 * qk_w + v_w:2 * qk_w + 2 * v_w])


def ret_proj(x_bf, w_in, cos, sin, tm):
    bsz, seq, d = x_bf.shape
    qk_w, v_w = H_C * DK_C, H_C * DV_C
    xmap = lambda b, i: (b, i, 0)
    tmap = lambda b, i: (i, 0)
    return pl.pallas_call(
        _ret_proj_kernel,
        grid=(bsz, seq // tm),
        in_specs=[pl.BlockSpec((1, tm, d), xmap), _const_spec((d, 2 * qk_w + 2 * v_w)),
                  pl.BlockSpec((tm, DK_C // 2), tmap), pl.BlockSpec((tm, DK_C // 2), tmap)],
        out_specs=[pl.BlockSpec((1, tm, qk_w), xmap), pl.BlockSpec((1, tm, qk_w), xmap),
                   pl.BlockSpec((1, tm, v_w), xmap), pl.BlockSpec((1, tm, v_w), xmap)],
        out_shape=[jax.ShapeDtypeStruct((bsz, seq, qk_w), BF16), jax.ShapeDtypeStruct((bsz, seq, qk_w), BF16),
                   jax.ShapeDtypeStruct((bsz, seq, v_w), BF16), jax.ShapeDtypeStruct((bsz, seq, v_w), F32)],
        compiler_params=_params("parallel", "parallel"),
        name="ret_proj",
    )(x_bf, w_in, cos, sin)


def _retention_kernel(lg_ref, q_ref, k_ref, v_ref, g_ref, gng_ref, gnb_ref, s0_ref, o_ref, sout_ref, s_sc,
                      *, t_true, n_heads):
    c = pl.program_id(2)

    @pl.when(c == 0)
    def _load_state():
        s_sc[...] = s0_ref[...]

    t = q_ref.shape[1]
    ti = lax.broadcasted_iota(jnp.int32, (t, t), 0)
    si = lax.broadcasted_iota(jnp.int32, (t, t), 1)
    diff = (ti - si).astype(F32)
    tcol = lax.broadcasted_iota(jnp.int32, (t, 1), 0).astype(F32)
    for hh in range(n_heads):
        lg = lg_ref[hh, :, 0:1]
        q = q_ref[0, :, hh * DK_C:(hh + 1) * DK_C]
        k = k_ref[0, :, hh * DK_C:(hh + 1) * DK_C]
        v = v_ref[0, :, hh * DV_C:(hh + 1) * DV_C]
        decay = jnp.where(diff >= 0, jnp.exp(jnp.maximum(diff, 0.0) * lg), 0.0)
        scores = lax.dot_general(q, k, _NT, preferred_element_type=F32) * decay
        o = _dot(scores.astype(BF16), v)
        state = s_sc[hh]
        o = o + _dot(q, state.astype(BF16)) * jnp.exp((tcol + 1.0) * lg)
        zeta = jnp.exp((t_true - 1.0 - tcol) * lg)
        kz = (k.astype(F32) * zeta).astype(BF16)
        new_state = jnp.exp(t_true * lg) * state + lax.dot_general(kz, v, _TN, preferred_element_type=F32)
        s_sc[hh] = new_state

        mu = jnp.mean(o, axis=-1, keepdims=True)
        oc = o - mu
        var = jnp.mean(oc * oc, axis=-1, keepdims=True)
        cols = slice(hh * DV_C, (hh + 1) * DV_C)
        on = oc * lax.rsqrt(var + GN_EPS) * gng_ref[:, cols] + gnb_ref[:, cols]
        o_ref[0, :, cols] = (jax.nn.silu(g_ref[0, :, cols]) * on).astype(BF16)

    @pl.when(c == pl.num_programs(2) - 1)
    def _store_state():
        sout_ref[...] = s_sc[...]


def retention(log_g, q, k, v, g, gn_g, gn_b, s0, chunk, t_true, n_heads):
    bsz, seq, _ = q.shape
    v_w = H_C * DV_C
    qmap = lambda b, h, c: (b, c, h)
    smap = lambda b, h, c: (b, h, 0, 0)
    hmap = lambda b, h, c: (0, h)
    return pl.pallas_call(
        functools.partial(_retention_kernel, t_true=float(t_true), n_heads=n_heads),
        grid=(bsz, H_C // n_heads, seq // chunk),
        in_specs=[pl.BlockSpec((n_heads, 1, LANES), lambda b, h, c: (h, 0, 0)),
                  pl.BlockSpec((1, chunk, n_heads * DK_C), qmap), pl.BlockSpec((1, chunk, n_heads * DK_C), qmap),
                  pl.BlockSpec((1, chunk, n_heads * DV_C), qmap), pl.BlockSpec((1, chunk, n_heads * DV_C), qmap),
                  pl.BlockSpec((1, n_heads * DV_C), hmap), pl.BlockSpec((1, n_heads * DV_C), hmap),
                  pl.BlockSpec((None, n_heads, DK_C, DV_C), smap)],
        out_specs=[pl.BlockSpec((1, chunk, n_heads * DV_C), qmap),
                   pl.BlockSpec((None, n_heads, DK_C, DV_C), smap)],
        out_shape=[jax.ShapeDtypeStruct((bsz, seq, v_w), BF16),
                   jax.ShapeDtypeStruct((bsz, H_C, DK_C, DV_C), F32)],
        scratch_shapes=[pltpu.VMEM((n_heads, DK_C, DV_C), F32)],
        compiler_params=_params("parallel", "parallel", "arbitrary"),
        name="retention",
    )(log_g, q, k, v, g, gn_g.reshape(1, v_w), gn_b.reshape(1, v_w), s0)


TM_DENSE = 512
TQ_ATTN = 512
T_RGLRU = 512
PAGE_GROUP = 8
RET_CHUNK = 256
ATTN_HEAD_PAIRS = 4
RET_HEADS = 4


def kernel(x_prompt, x_sample, cache_k_a, cache_v_a, cache_logf_a, page_table, state_conv_b, state_h_b, state_ret_c, w_in_a, b_f_a, w_out_a, w_in_b, conv_w_b, conv_b_b, w_gate_b, b_gate_b, lam_b, w_out_b, w_in_c, gn_g_c, gn_b_c, w_out_c, ln1_g, ln1_b, w_up_ffn, w_down_ffn, ln2_g, ln2_b):
    bsz, seq, d = x_prompt.shape
    bd, n_new, _ = x_sample.shape
    mp, ms = bsz * seq, bd * n_new
    past_len = page_table.shape[1] * cache_k_a.shape[2]

    xp = x_prompt.reshape(mp, d)
    xs = x_sample.reshape(ms, d)
    xp_bf = None
    xs_bf = xs.astype(BF16)

    n_la, pool, page = cache_k_a.shape[:3]
    kT_pool = jnp.transpose(cache_k_a, (0, 1, 3, 4, 2)).reshape(n_la, pool, d, page)
    vT_pool = jnp.transpose(cache_v_a, (0, 1, 3, 4, 2)).reshape(n_la, pool, d, page)
    lfT_pool = jnp.transpose(cache_logf_a, (0, 1, 3, 2))

    log_g = jnp.log1p(-jnp.exp2(-5.0 - jnp.arange(H_C, dtype=F32)))
    log_g = jnp.broadcast_to(log_g[:, None, None], (H_C, 1, LANES))

    outs_a = {name: [] for name in ("lfp", "ks", "vs", "lfs")}
    out_b, out_c = {}, {}
    kv_all = None

    for i in range(DEPTH):
        jl, kind = divmod(i, 3)
        if kind == 0:
            w = w_in_a[jl]
            wq = w[:, 0:d].astype(BF16)
            wkT = w[:, d:2 * d].T.astype(BF16)
            wvT = w[:, 2 * d:3 * d].T.astype(BF16)
            wfT = w[:, 3 * d:].T.astype(BF16)
            w_out = w_out_a[jl].astype(BF16)
            q_bf, kT_all, vT_all, kT_bf, vT_bf, lfT = fox_proj(
                (xp if xp_bf is None else xp_bf).reshape(bsz, seq, d), wq, wkT, wvT, wfT, b_f_a[jl].reshape(H_A, 1), TM_DENSE,
                jl, n_la, kv_all)
            kv_all = (kT_all, vT_all)
            ct = fox_cumsum(lfT)
            o_bf = fox_attn(q_bf, kT_bf, vT_bf, jnp.transpose(ct, (0, 2, 1)), ct, TQ_ATTN, ATTN_HEAD_PAIRS)
            mix_p = (o_bf.reshape(mp, d), w_out)
            outs_a["lfp"].append(lfT)
            wf_pad = jnp.pad(w[:, 3 * d:], ((0, 0), (0, LANES - H_A))).astype(BF16)
            bf_pad = jnp.pad(b_f_a[jl], (0, LANES - H_A)).reshape(1, LANES)
            qs, ks, vs, lfs = fox_proj_sample(xs_bf, w[:, 0:3 * d].astype(BF16), wf_pad, bf_pad)
            lfs = lfs[:, :H_A].reshape(bd, n_new, H_A)
            lfs_T = jnp.pad(jnp.transpose(lfs, (0, 2, 1)), ((0, 0), (0, 0), (0, LANES - n_new)))
            o_s = fox_sample_attn(page_table, kT_pool, vT_pool, lfT_pool, jl, qs.reshape(bd, n_new, d),
                                  ks.reshape(bd, n_new, d), vs.reshape(bd, n_new, d), lfs_T, PAGE_GROUP)
            mix_s = (o_s.reshape(ms, d).astype(BF16), w_out)
            outs_a["ks"].append(ks.reshape(bd, n_new, H_A, DH_A))
            outs_a["vs"].append(vs.reshape(bd, n_new, H_A, DH_A))
            outs_a["lfs"].append(lfs)
        elif kind == 1:
            w_in = w_in_b[jl].astype(BF16)
            w_out = w_out_b[jl].astype(BF16)
            wband = _band_gate_weights(w_gate_b[jl])
            xp3, xp_bf3, tail, hlast = rglru_prompt(
                xp_bf.reshape(bsz, seq, d), xp.reshape(bsz, seq, d), w_in, conv_w_b[jl], conv_b_b[jl], wband,
                b_gate_b[jl], lam_b[jl], w_out, ln1_g[i], ln1_b[i], T_RGLRU)
            xp, xp_bf = xp3.reshape(mp, d), xp_bf3.reshape(mp, d)
            out_b["cp"] = tail[:, SUBLANES - (CONV_W - 1):, :]
            out_b["hp"] = hlast[:, 0, :]
            to_tm = lambda a: jnp.transpose(a.reshape(bd, n_new, d), (1, 0, 2)).reshape(ms, d)
            ys_tm, newst, hs = rglru_sample(
                to_tm(xs_bf), to_tm(xs), jnp.transpose(state_conv_b[jl], (1, 0, 2)), state_h_b[jl], w_in,
                conv_w_b[jl], conv_b_b[jl], wband, b_gate_b[jl], lam_b[jl], w_out, ln1_g[i], ln1_b[i])
            xs = jnp.transpose(ys_tm.reshape(n_new, bd, d), (1, 0, 2)).reshape(ms, d)
            xs_bf = xs.astype(BF16)
            out_b["cs"] = jnp.transpose(newst, (1, 0, 2))
            out_b["hs"] = hs
            mix_p = mix_s = None
        else:
            w_in = w_in_c[jl].astype(BF16)
            w_out = w_out_c[jl].astype(BF16)
            v_w = H_C * DV_C
            cos_p, sin_p = rope_tables(seq, 0, seq)
            q, k, v, g = ret_proj(xp_bf.reshape(bsz, seq, d), w_in, cos_p, sin_p, TM_DENSE)
            chunk = next((c for c in (RET_CHUNK, CHUNK_C) if seq % c == 0), seq)
            og, s_p = retention(log_g, q, k, v, g, gn_g_c[jl], gn_b_c[jl],
                                jnp.zeros((bsz, H_C, DK_C, DV_C), F32), chunk, chunk, RET_HEADS)
            mix_p = (og.reshape(mp, v_w), w_out)
            out_c["sp"] = s_p
            cos_s, sin_s = rope_tables(ms, past_len, n_new)
            q, k, v, g = ret_proj(xs_bf.reshape(1, ms, d), w_in, cos_s, sin_s, ms)
            pad_rows = lambda a: jnp.pad(a.reshape(bd, n_new, a.shape[-1]), ((0, 0), (0, SUBLANES - n_new), (0, 0)))
            og, s_s = retention(log_g, pad_rows(q), pad_rows(k), pad_rows(v), pad_rows(g), gn_g_c[jl], gn_b_c[jl],
                                state_ret_c[jl], SUBLANES, n_new, RET_HEADS)
            mix_s = (og[:, :n_new].reshape(ms, v_w), w_out)
            out_c["ss"] = s_s
        if mix_p is not None:
            xp, xp_bf = proj_ln(mix_p[0], mix_p[1], xp, ln1_g[i], ln1_b[i], TM_DENSE)
            xs, xs_bf = proj_ln(mix_s[0], mix_s[1], xs, ln1_g[i], ln1_b[i], TM_DENSE)
        w_up = w_up_ffn[i].astype(BF16)
        w_dn = w_down_ffn[i].astype(BF16)
        xp, xp_bf = ffn_ln(xp_bf, xp, w_up, w_dn, ln2_g[i], ln2_b[i], TM_DENSE)
        xs, xs_bf = ffn_ln(xs_bf, xs, w_up, w_dn, ln2_g[i], ln2_b[i], TM_DENSE)

    kv_out = lambda a: jnp.transpose(a.reshape(n_la, bsz, H_A, DH_A, seq), (0, 1, 4, 2, 3))
    return (xp.reshape(bsz, seq, d), xs.reshape(bd, n_new, d),
            kv_out(kv_all[0]), kv_out(kv_all[1]), jnp.transpose(jnp.stack(outs_a["lfp"]), (0, 1, 3, 2)),
            jnp.stack(outs_a["ks"]), jnp.stack(outs_a["vs"]), jnp.stack(outs_a["lfs"]),
            out_b["cp"][None], out_b["hp"][None], out_b["cs"][None], out_b["hs"][None],
            out_c["sp"][None], out_c["ss"][None])
```

```python
import functools
import math

import jax
import jax.numpy as jnp
from jax import lax
from jax.experimental import pallas as pl
from jax.experimental.pallas import tpu as pltpu

F32 = jnp.float32
BF16 = jnp.bfloat16

D_MODEL = 1024
DEPTH = 4
H_A = 16
DH_A = D_MODEL // H_A
D_RNN = 1408
N_BLK_B = 16
BW_B = D_RNN // N_BLK_B
CONV_W = 4
LRU_C = 8.0
H_C = 4
DK_C = D_MODEL // H_C
DV_C = 2 * DK_C
CHUNK_C = 128
ROPE_BASE = 10000.0
D_FF = 2816
ALPHA = (2 * DEPTH) ** 0.25
LN_EPS = 1e-5
GN_EPS = 1e-5
LOG2E = math.log2(math.e)

LANES = 128
SUBLANES = 8
MXU_COLS = 256
VMEM_LIMIT = 56 * 1024 * 1024

_NT = (((1,), (1,)), ((), ()))
_TN = (((0,), (0,)), ((), ()))


def _params(*sem):
    return pltpu.CompilerParams(dimension_semantics=sem, vmem_limit_bytes=VMEM_LIMIT)


def _const_spec(shape):
    zeros = (0,) * len(shape)
    return pl.BlockSpec(shape, lambda *_: zeros, pipeline_mode=pl.Buffered(1))


def _dot(a, b):
    return jnp.dot(a, b, preferred_element_type=F32)


def _layer_norm(z, g, b):
    mu = jnp.mean(z, axis=-1, keepdims=True)
    zc = z - mu
    var = jnp.mean(zc * zc, axis=-1, keepdims=True)
    return zc * lax.rsqrt(var + LN_EPS) * g + b


def _log_sigmoid(z):
    return jnp.minimum(z, 0.0) - jnp.log1p(jnp.exp(-jnp.abs(z)))


def _split3(x):
    hi = x.astype(BF16)
    r1 = x - hi.astype(F32)
    mid = r1.astype(BF16)
    lo = (r1 - mid.astype(F32)).astype(BF16)
    return hi, mid, lo


def _dot3(pieces, m01):
    hi, mid, lo = pieces
    return _dot(hi, m01) + _dot(mid, m01) + _dot(lo, m01)


def _dot_f32_by_01(x, m01):
    return _dot3(_split3(x), m01)


def _tri01(n, strict_lower_rows):
    r = lax.broadcasted_iota(jnp.int32, (n, n), 0)
    c = lax.broadcasted_iota(jnp.int32, (n, n), 1)
    keep = (r > c) if strict_lower_rows else (r <= c)
    return jnp.where(keep, 1.0, 0.0).astype(BF16)


def _proj_ln_kernel(a_ref, w_ref, x_ref, g_ref, b_ref, y_ref, ybf_ref):
    m = _dot(a_ref[...], w_ref[...])
    y = _layer_norm(ALPHA * x_ref[...] + m, g_ref[...], b_ref[...])
    y_ref[...] = y
    ybf_ref[...] = y.astype(BF16)


def proj_ln(a_bf, w_bf, x, g, b, tm):
    m, k = a_bf.shape
    d = w_bf.shape[1]
    tm = min(tm, m)
    row = lambda i: (i, 0)
    return pl.pallas_call(
        _proj_ln_kernel,
        grid=(m // tm,),
        in_specs=[pl.BlockSpec((tm, k), row), _const_spec((k, d)), pl.BlockSpec((tm, d), row),
                  _const_spec((1, d)), _const_spec((1, d))],
        out_specs=[pl.BlockSpec((tm, d), row), pl.BlockSpec((tm, d), row)],
        out_shape=[jax.ShapeDtypeStruct((m, d), F32), jax.ShapeDtypeStruct((m, d), BF16)],
        compiler_params=_params("parallel"),
        name="proj_ln",
    )(a_bf, w_bf, x, g.reshape(1, d), b.reshape(1, d))


def _ffn_ln_kernel(xbf_ref, x_ref, wup_ref, wdn_ref, g_ref, b_ref, y_ref, ybf_ref, *, n_chunks):
    xb = xbf_ref[...]
    cw = D_FF // n_chunks
    acc = None
    for c in range(n_chunks):
        gate = _dot(xb, wup_ref[:, c * cw:(c + 1) * cw])
        up = _dot(xb, wup_ref[:, D_FF + c * cw:D_FF + (c + 1) * cw])
        h = (jax.nn.silu(gate) * up).astype(BF16)
        part = _dot(h, wdn_ref[c * cw:(c + 1) * cw, :])
        acc = part if acc is None else acc + part
    y = _layer_norm(ALPHA * x_ref[...] + acc, g_ref[...], b_ref[...])
    y_ref[...] = y
    ybf_ref[...] = y.astype(BF16)


def ffn_ln(x_bf, x, w_up_bf, w_dn_bf, layer, g, b, tm, n_chunks=1):
    m, d = x.shape
    tm = min(tm, m)
    row = lambda i: (i, 0)
    layer_spec = lambda r, c: pl.BlockSpec((None, r, c), lambda i: (layer, 0, 0), pipeline_mode=pl.Buffered(1))
    return pl.pallas_call(
        functools.partial(_ffn_ln_kernel, n_chunks=n_chunks),
        grid=(m // tm,),
        in_specs=[pl.BlockSpec((tm, d), row), pl.BlockSpec((tm, d), row),
                  layer_spec(d, 2 * D_FF), layer_spec(D_FF, d),
                  _const_spec((1, d)), _const_spec((1, d))],
        out_specs=[pl.BlockSpec((tm, d), row), pl.BlockSpec((tm, d), row)],
        out_shape=[jax.ShapeDtypeStruct((m, d), F32), jax.ShapeDtypeStruct((m, d), BF16)],
        compiler_params=_params("parallel"),
        name="ffn_ln",
    )(x_bf, x, w_up_bf, w_dn_bf, g.reshape(1, d), b.reshape(1, d))


def _fox_proj_kernel(*refs, layer, n_layers, first):
    if first:
        x_ref, wq_ref, wkT_ref, wvT_ref, wfT_ref, bf_ref = refs[:6]
    else:
        x_ref, wq_ref, wkT_ref, wvT_ref, wfT_ref, bf_ref, _, _ = refs[:8]
    q_ref, kT_ref, vT_ref, kTb_ref, vTb_ref, lfT_ref = refs[-6:]
    x = x_ref[0].astype(BF16)
    q_ref[0] = (_dot(x, wq_ref[...]) * (DH_A ** -0.5 * LOG2E)).astype(BF16)
    kT = lax.dot_general(wkT_ref[...], x, _NT, preferred_element_type=F32)
    vT = lax.dot_general(wvT_ref[...], x, _NT, preferred_element_type=F32)
    if first:
        for l in range(n_layers):
            kT_ref[l, 0] = kT if l == layer else jnp.zeros_like(kT)
            vT_ref[l, 0] = vT if l == layer else jnp.zeros_like(vT)
    else:
        kT_ref[0] = kT
        vT_ref[0] = vT
    kTb_ref[0] = kT.astype(BF16)
    vTb_ref[0] = vT.astype(BF16)
    f = lax.dot_general(wfT_ref[...], x, _NT, preferred_element_type=F32)
    lfT_ref[0] = _log_sigmoid(f + bf_ref[...])


def fox_proj(x_bf, wq, wkT, wvT, wfT, bf_col, tm, layer, n_layers, kv_all=None):
    bsz, seq, d = x_bf.shape
    first = kv_all is None
    xmap = lambda b, i: (b, i, 0)
    tmap = lambda b, i: (b, 0, i)
    if first:
        kv_spec = pl.BlockSpec((n_layers, 1, d, tm), lambda b, i: (0, b, 0, i))
        extra_in, extra_specs, aliases = (), [], {}
    else:
        kv_spec = pl.BlockSpec((None, 1, d, tm), lambda b, i: (layer, b, 0, i))
        extra_in = tuple(kv_all)
        extra_specs = [pl.BlockSpec(memory_space=pl.ANY)] * 2
        aliases = {6: 1, 7: 2}
    kv_shape = jax.ShapeDtypeStruct((n_layers, bsz, d, seq), F32)
    return pl.pallas_call(
        functools.partial(_fox_proj_kernel, layer=layer, n_layers=n_layers, first=first),
        grid=(bsz, seq // tm),
        in_specs=[pl.BlockSpec((1, tm, d), xmap), _const_spec((d, d)), _const_spec((d, d)),
                  _const_spec((d, d)), _const_spec((H_A, d)), _const_spec((H_A, 1))] + extra_specs,
        out_specs=[pl.BlockSpec((1, tm, d), xmap), kv_spec, kv_spec, pl.BlockSpec((1, d, tm), tmap),
                   pl.BlockSpec((1, d, tm), tmap), pl.BlockSpec((1, H_A, tm), tmap)],
        out_shape=[jax.ShapeDtypeStruct((bsz, seq, d), BF16), kv_shape, kv_shape,
                   jax.ShapeDtypeStruct((bsz, d, seq), BF16), jax.ShapeDtypeStruct((bsz, d, seq), BF16),
                   jax.ShapeDtypeStruct((bsz, H_A, seq), F32)],
        input_output_aliases=aliases,
        compiler_params=_params("parallel", "parallel"),
        name="fox_proj",
    )(x_bf, wq, wkT, wvT, wfT, bf_col, *extra_in)


def _fox_cumsum_kernel(lf_ref, ct_ref, *, chunk):
    seq = lf_ref.shape[2]
    prefix = _tri01(chunk, strict_lower_rows=False)
    carry = jnp.zeros((H_A, 1), F32)
    for i in range(seq // chunk):
        cs = _dot_f32_by_01(lf_ref[0, :, i * chunk:(i + 1) * chunk], prefix) + carry
        ct_ref[0, :, i * chunk:(i + 1) * chunk] = cs
        carry = cs[:, chunk - 1:chunk]


def fox_cumsum(lfT):
    bsz, h, seq = lfT.shape
    chunk = min(seq, MXU_COLS)
    spec = pl.BlockSpec((1, h, seq), lambda b: (b, 0, 0))
    return pl.pallas_call(
        functools.partial(_fox_cumsum_kernel, chunk=chunk),
        grid=(bsz,), in_specs=[spec], out_specs=spec,
        out_shape=jax.ShapeDtypeStruct((bsz, h, seq), F32),
        compiler_params=_params("parallel"),
        name="fox_cumsum",
    )(lfT)


def _fox_attn_kernel(qi_ref, kj_ref, q_ref, kT_ref, vT_ref, c_ref, ct_ref, o_ref, cq_sc, m_sc, l_sc, acc_sc,
                     *, tq, n_pairs):
    g = pl.program_id(1)
    i = qi_ref[pl.program_id(2)]
    j = kj_ref[pl.program_id(2)]
    tk = kT_ref.shape[2]

    def spread(x, width):
        return jnp.concatenate([x] * (width // LANES), axis=1) if width >= LANES else x[:, 0:width]

    @pl.when(j == 0)
    def _init():
        cblk = c_ref[0]
        hl = lax.broadcasted_iota(jnp.int32, cblk.shape, 1)
        for hh in range(2 * n_pairs):
            cq = jnp.sum(jnp.where(hl == 2 * n_pairs * g + hh, cblk, 0.0), axis=1, keepdims=True)
            cq_sc[hh] = jnp.broadcast_to(cq * LOG2E, (tq, LANES))
        m_sc[...] = jnp.full(m_sc.shape, -jnp.inf, F32)
        l_sc[...] = jnp.zeros(l_sc.shape, F32)
        acc_sc[...] = jnp.zeros(acc_sc.shape, F32)

    def step(row_lo, n_rows, n_keys, masked):
        rows = slice(row_lo, row_lo + n_rows)
        first_head = lax.broadcasted_iota(jnp.int32, (n_rows, LANES), 1) < DH_A
        for pp in range(n_pairs):
            q = q_ref[0, rows, pp * LANES:(pp + 1) * LANES]
            kT = kT_ref[0, pp * LANES:(pp + 1) * LANES, 0:n_keys]
            vT = vT_ref[0, pp * LANES:(pp + 1) * LANES, 0:n_keys]
            alphas, pvs = [], []
            for a in range(2):
                hh = 2 * pp + a
                qa = jnp.where(first_head if a == 0 else jnp.logical_not(first_head), q, jnp.zeros_like(q))
                ck = ct_ref[0, pl.ds(2 * n_pairs * g + hh, 1), 0:n_keys] * LOG2E
                y = _dot(qa, kT) - ck
                if masked:
                    r = lax.broadcasted_iota(jnp.int32, y.shape, 0) + row_lo
                    c = lax.broadcasted_iota(jnp.int32, y.shape, 1)
                    y = jnp.where(r >= c, y, -jnp.inf)
                cq = cq_sc[hh, rows]
                m_old = m_sc[hh, rows]
                m_new = jnp.maximum(m_old, jnp.max(y, axis=1, keepdims=True) + cq)
                alpha = jnp.exp2(m_old - m_new)
                pr = jnp.exp2(y - spread(m_new - cq, n_keys))
                l_sc[hh, rows] = alpha * l_sc[hh, rows] + jnp.sum(pr, axis=1, keepdims=True)
                m_sc[hh, rows] = m_new
                pvs.append(lax.dot_general(pr.astype(BF16), vT, _NT, preferred_element_type=F32))
                alphas.append(alpha)
            acc_sc[pp, rows] = (acc_sc[pp, rows] * jnp.where(first_head, alphas[0], alphas[1])
                                + jnp.where(first_head, pvs[0], pvs[1]))

    @pl.when(j < i)
    def _below():
        step(0, tq, tk, False)

    @pl.when(j == i)
    def _diag():
        step(0, tq // 2, tk // 2, True)
        step(tq // 2, tq // 2, tk, True)
        first_head = lax.broadcasted_iota(jnp.int32, (tq, LANES), 1) < DH_A
        for pp in range(n_pairs):
            norm = jnp.where(first_head, l_sc[2 * pp], l_sc[2 * pp + 1])
            o_ref[0, :, pp * LANES:(pp + 1) * LANES] = (acc_sc[pp] / norm).astype(BF16)


def fox_attn(q_bf, kT_bf, vT_bf, c, ct, tq, n_pairs):
    bsz, seq, d = q_bf.shape
    nq = seq // tq
    wid = n_pairs * LANES
    pairs = [(i, j) for i in range(nq) for j in range(i + 1)]
    qi = jnp.asarray([ij[0] for ij in pairs], jnp.int32)
    kj = jnp.asarray([ij[1] for ij in pairs], jnp.int32)
    qmap = lambda b, g, s, qi, kj: (b, qi[s], g)
    kmap = lambda b, g, s, qi, kj: (b, g, kj[s])
    grid_spec = pltpu.PrefetchScalarGridSpec(
        num_scalar_prefetch=2,
        grid=(bsz, d // wid, len(pairs)),
        in_specs=[pl.BlockSpec((1, tq, wid), qmap), pl.BlockSpec((1, wid, tq), kmap),
                  pl.BlockSpec((1, wid, tq), kmap),
                  pl.BlockSpec((1, tq, H_A), lambda b, g, s, qi, kj: (b, qi[s], 0)),
                  pl.BlockSpec((1, H_A, tq), lambda b, g, s, qi, kj: (b, 0, kj[s]))],
        out_specs=pl.BlockSpec((1, tq, wid), qmap),
        scratch_shapes=[pltpu.VMEM((2 * n_pairs, tq, LANES), F32), pltpu.VMEM((2 * n_pairs, tq, LANES), F32),
                        pltpu.VMEM((2 * n_pairs, tq, LANES), F32), pltpu.VMEM((n_pairs, tq, LANES), F32)],
    )
    return pl.pallas_call(
        functools.partial(_fox_attn_kernel, tq=tq, n_pairs=n_pairs),
        grid_spec=grid_spec,
        out_shape=jax.ShapeDtypeStruct((bsz, seq, d), BF16),
        compiler_params=_params("parallel", "parallel", "arbitrary"),
        name="fox_attn",
    )(qi, kj, q_bf, kT_bf, vT_bf, c, ct)


def _fox_proj_sample_kernel(x_ref, w_ref, wf_ref, bf_ref, q_ref, k_ref, v_ref, lf_ref):
    x = x_ref[...]
    d = x.shape[1]
    q_ref[...] = _dot(x, w_ref[:, 0:d])
    k_ref[...] = _dot(x, w_ref[:, d:2 * d])
    v_ref[...] = _dot(x, w_ref[:, 2 * d:3 * d])
    lf_ref[...] = _log_sigmoid(_dot(x, wf_ref[...]) + bf_ref[...])


def fox_proj_sample(x_bf, w_qkv, wf_pad, bf_pad):
    m, d = x_bf.shape
    full = lambda shape: pl.BlockSpec(shape, lambda: (0,) * len(shape))
    return pl.pallas_call(
        _fox_proj_sample_kernel,
        in_specs=[full((m, d)), full((d, 3 * d)), full((d, LANES)), full((1, LANES))],
        out_specs=[full((m, d)), full((m, d)), full((m, d)), full((m, LANES))],
        out_shape=[jax.ShapeDtypeStruct((m, d), F32)] * 3 + [jax.ShapeDtypeStruct((m, LANES), F32)],
        compiler_params=pltpu.CompilerParams(vmem_limit_bytes=VMEM_LIMIT),
        name="fox_proj_sample",
    )(x_bf, w_qkv, wf_pad, bf_pad)


def _fox_sample_attn_kernel(pt_ref, *refs, n_new, group):
    kT_refs, vT_refs, lf_refs = refs[0:group], refs[group:2 * group], refs[2 * group:3 * group]
    (q_ref, kn_ref, vn_ref, lfs_ref, o_ref,
     qbd_sc, ctn_sc, ctcol_sc, m_sc, l_sc, acc_sc, carry_sc) = refs[3 * group:]
    j = pl.program_id(1)
    rows = n_new * H_A
    d = qbd_sc.shape[1]
    page = kT_refs[0].shape[1]

    def head_diag():
        r = lax.broadcasted_iota(jnp.int32, (rows, d), 0)
        c = lax.broadcasted_iota(jnp.int32, (rows, d), 1)
        return (c // DH_A) == (r % H_A)

    def spread(x, width):
        return jnp.concatenate([x] * (width // LANES), axis=1) if width >= LANES else x[:, 0:width]

    def online_update(s, pv_fn):
        m_old = m_sc[...]
        m_new = jnp.maximum(m_old, jnp.max(s, axis=1, keepdims=True))
        alpha = jnp.exp(m_old - m_new)
        pr = jnp.exp(s - spread(m_new, s.shape[1]))
        l_sc[...] = alpha * l_sc[...] + jnp.sum(pr, axis=1, keepdims=True)
        m_sc[...] = m_new
        acc_sc[...] = spread(alpha, d) * acc_sc[...] + pv_fn(pr.astype(BF16))

    @pl.when(j == 0)
    def _init():
        q = q_ref[0]
        qe = jnp.concatenate([jnp.broadcast_to(q[t:t + 1, :], (H_A, d)) for t in range(n_new)], axis=0)
        qbd_sc[...] = jnp.where(head_diag(), qe * DH_A ** -0.5, 0.0).astype(BF16)
        ctn = _dot_f32_by_01(lfs_ref[0], _tri01(LANES, strict_lower_rows=False))
        ctn_sc[...] = ctn
        ctcol = jnp.concatenate([ctn[:, t:t + 1] for t in range(n_new)], axis=0)
        ctcol_sc[...] = jnp.broadcast_to(ctcol, ctcol_sc.shape)
        m_sc[...] = jnp.full(m_sc.shape, -jnp.inf, F32)
        l_sc[...] = jnp.zeros(l_sc.shape, F32)
        acc_sc[...] = jnp.zeros(acc_sc.shape, F32)
        carry_sc[...] = jnp.zeros(carry_sc.shape, F32)

    qbd = qbd_sc[...]
    lf_all = jnp.concatenate([lf_refs[g][...] for g in range(group)], axis=0)
    suffix_all = _dot_f32_by_01(lf_all, _tri01(page, strict_lower_rows=True))
    carry = carry_sc[...]
    parts = []
    for g in range(group):
        bias = suffix_all[g * H_A:(g + 1) * H_A] + carry
        carry = carry + jnp.sum(lf_refs[g][...], axis=1, keepdims=True)
        parts.append(_dot(qbd, kT_refs[g][...].astype(BF16)) + jnp.concatenate([bias] * n_new, axis=0))
    carry_sc[...] = carry
    s = jnp.concatenate(parts, axis=1) + spread(ctcol_sc[...], group * page)

    def pv_pages(p):
        out = None
        for g in range(group):
            t = lax.dot_general(p[:, g * page:(g + 1) * page], vT_refs[g][...].astype(BF16), _NT,
                                preferred_element_type=F32)
            out = t if out is None else out + t
        return out

    online_update(s, pv_pages)

    @pl.when(j == pl.num_programs(1) - 1)
    def _finish():
        pad = jnp.zeros((SUBLANES - n_new, d), F32)
        kn = jnp.concatenate([kn_ref[0], pad], axis=0).astype(BF16)
        vn = jnp.concatenate([vn_ref[0], pad], axis=0).astype(BF16)
        sn = lax.dot_general(qbd_sc[...], kn, _NT, preferred_element_type=F32)
        ctk = jnp.concatenate([ctn_sc[:, 0:SUBLANES]] * n_new, axis=0)
        tq = lax.broadcasted_iota(jnp.int32, sn.shape, 0) // H_A
        tk = lax.broadcasted_iota(jnp.int32, sn.shape, 1)
        sn = jnp.where(tk <= tq, sn + (ctcol_sc[:, 0:SUBLANES] - ctk), -jnp.inf)
        online_update(sn, lambda p: _dot(p, vn))
        om = jnp.where(head_diag(), acc_sc[...] / spread(l_sc[...], d), 0.0)
        o_ref[0] = jnp.concatenate(
            [jnp.sum(om[t * H_A:(t + 1) * H_A], axis=0, keepdims=True) for t in range(n_new)], axis=0)


def fox_sample_attn(page_table, kT_pool, vT_pool, lfT_pool, layer, q, k_new, v_new, lfs_T, group):
    bd, n_new, d = q.shape
    n_pages = page_table.shape[1]
    page = kT_pool.shape[3]
    rows = n_new * H_A
    group = math.gcd(group, n_pages)
    pmaps = [lambda b, j, pt, g=g: (layer, pt[b, n_pages - 1 - (j * group + g)], 0, 0) for g in range(group)]
    bmap = lambda b, j, pt: (b, 0, 0)
    grid_spec = pltpu.PrefetchScalarGridSpec(
        num_scalar_prefetch=1,
        grid=(bd, n_pages // group),
        in_specs=([pl.BlockSpec((None, None, d, page), pm) for pm in pmaps]
                  + [pl.BlockSpec((None, None, d, page), pm) for pm in pmaps]
                  + [pl.BlockSpec((None, None, H_A, page), pm) for pm in pmaps]
                  + [pl.BlockSpec((1, n_new, d), bmap), pl.BlockSpec((1, n_new, d), bmap),
                     pl.BlockSpec((1, n_new, d), bmap), pl.BlockSpec((1, H_A, LANES), bmap)]),
        out_specs=pl.BlockSpec((1, n_new, d), bmap),
        scratch_shapes=[pltpu.VMEM((rows, d), BF16), pltpu.VMEM((H_A, LANES), F32),
                        pltpu.VMEM((rows, LANES), F32), pltpu.VMEM((rows, LANES), F32),
                        pltpu.VMEM((rows, LANES), F32), pltpu.VMEM((rows, d), F32), pltpu.VMEM((H_A, LANES), F32)],
    )
    return pl.pallas_call(
        functools.partial(_fox_sample_attn_kernel, n_new=n_new, group=group),
        grid_spec=grid_spec,
        out_shape=jax.ShapeDtypeStruct((bd, n_new, d), F32),
        compiler_params=_params("parallel", "arbitrary"),
        name="fox_sample_attn",
    )(page_table, *([kT_pool] * group), *([vT_pool] * group), *([lfT_pool] * group), q, k_new, v_new, lfs_T)


GATE_TILE = MXU_COLS
GATE_WIN = 2 * MXU_COLS
N_GATE_TILES = -(-D_RNN // GATE_TILE)


def _gate_window_start(j):
    first_block = (j * GATE_TILE) // BW_B
    return min((first_block * BW_B) // LANES * LANES, D_RNN - GATE_WIN)


def _band_gate_weights(w_gate):
    eye = jnp.eye(N_BLK_B, dtype=w_gate.dtype)
    tiles = []
    for part in range(2):
        wp = w_gate[:, :, part * BW_B:(part + 1) * BW_B]
        dense = (eye[:, None, :, None] * wp[:, :, None, :]).reshape(D_RNN, D_RNN)
        dense = jnp.pad(dense, ((0, 0), (0, N_GATE_TILES * GATE_TILE - D_RNN)))
        tiles.append(jnp.stack([
            dense[_gate_window_start(j):_gate_window_start(j) + GATE_WIN, j * GATE_TILE:(j + 1) * GATE_TILE]
            for j in range(N_GATE_TILES)]))
    return jnp.stack(tiles).astype(BF16)


def _rglru_gate_tile(j, xc, xc_bf, wband_ref, bg_ref, lam_ref):
    lo = j * GATE_TILE
    w = min(GATE_TILE, D_RNN - lo)
    ks = _gate_window_start(j)
    xw = xc_bf[:, ks:ks + GATE_WIN]
    r = jax.nn.sigmoid(_dot(xw, wband_ref[0, j])[:, :w] + bg_ref[0:1, lo:lo + w])
    ig = jax.nn.sigmoid(_dot(xw, wband_ref[1, j])[:, :w] + bg_ref[1:2, lo:lo + w])
    log_a = LRU_C * r * _log_sigmoid(lam_ref[:, lo:lo + w])
    a = jnp.exp(log_a)
    u = jnp.sqrt(-jnp.tanh(log_a) * (a * a + 1.0)) * (ig * xc[:, lo:lo + w])
    return a, u


def _rglru_prompt_kernel(xbf_ref, x_ref, win_ref, cw_ref, cb_ref, wband_ref, bg_ref, lam_ref, wout_ref,
                         g_ref, b_ref, y_ref, ybf_ref, tail_ref, hlast_ref, tail_sc, h_sc, a_sc, u_sc):
    t_rows = xbf_ref.shape[1]

    @pl.when(pl.program_id(1) == 0)
    def _reset():
        tail_sc[...] = jnp.zeros(tail_sc.shape, F32)
        h_sc[...] = jnp.zeros(h_sc.shape, F32)

    xb16 = xbf_ref[0]
    gate_br = _dot(xb16, win_ref[:, 0:D_RNN])
    xb = _dot(xb16, win_ref[:, D_RNN:2 * D_RNN])

    row8 = lax.broadcasted_iota(jnp.int32, (SUBLANES, D_RNN), 0)
    tail = tail_sc[...]
    xc = cb_ref[...] + cw_ref[CONV_W - 1:CONV_W, :] * xb
    for k in range(1, CONV_W):
        xs = pltpu.roll(xb, k, axis=0)
        head = jnp.where(row8 < k, pltpu.roll(tail, k, axis=0), xs[0:SUBLANES])
        xs = jnp.concatenate([head, xs[SUBLANES:]], axis=0)
        xc = xc + cw_ref[CONV_W - 1 - k:CONV_W - k, :] * xs
    new_tail = xb[t_rows - SUBLANES:t_rows]
    tail_sc[...] = new_tail
    tail_ref[0] = new_tail
    xc_bf = xc.astype(BF16)

    for j in range(N_GATE_TILES):
        lo = j * GATE_TILE
        w = min(GATE_TILE, D_RNN - lo)
        a, u = _rglru_gate_tile(j, xc, xc_bf, wband_ref, bg_ref, lam_ref)
        groups = (t_rows // SUBLANES, SUBLANES, w)
        a = a.reshape(groups)
        u = u.reshape(groups)
        pos = lax.broadcasted_iota(jnp.int32, groups, 1)
        for s in (1, 2, 4):
            live = pos >= s
            u = jnp.where(live, a * pltpu.roll(u, s, axis=1) + u, u)
            a = jnp.where(live, a * pltpu.roll(a, s, axis=1), a)
        a_sc[:, lo:lo + w] = a.reshape(t_rows, w)
        u_sc[:, lo:lo + w] = u.reshape(t_rows, w)

    def group(i, h_prev):
        blk = pl.ds(pl.multiple_of(i * SUBLANES, SUBLANES), SUBLANES)
        hb = a_sc[blk, :] * h_prev + u_sc[blk, :]
        u_sc[blk, :] = hb
        return hb[SUBLANES - 1:SUBLANES, :]

    h_sc[...] = lax.fori_loop(0, t_rows // SUBLANES, group, h_sc[...], unroll=2)

    hlast_ref[0] = jnp.broadcast_to(h_sc[...], (SUBLANES, D_RNN))
    mixed = (u_sc[...] * jax.nn.gelu(gate_br)).astype(BF16)
    y = _layer_norm(ALPHA * x_ref[0] + _dot(mixed, wout_ref[...]), g_ref[...], b_ref[...])
    y_ref[0] = y
    ybf_ref[0] = y.astype(BF16)


def rglru_prompt(x_bf, x, w_in, conv_w, conv_b, wband, b_gate, lam, w_out, g, b, t_rows):
    bsz, seq, d = x.shape
    xmap = lambda bi, ti: (bi, ti, 0)
    smap = lambda bi, ti: (bi, 0, 0)
    return pl.pallas_call(
        _rglru_prompt_kernel,
        grid=(bsz, seq // t_rows),
        in_specs=[pl.BlockSpec((1, t_rows, d), xmap), pl.BlockSpec((1, t_rows, d), xmap),
                  _const_spec((d, 2 * D_RNN)), _const_spec((CONV_W, D_RNN)), _const_spec((1, D_RNN)),
                  _const_spec((2, N_GATE_TILES, GATE_WIN, GATE_TILE)), _const_spec((2, D_RNN)),
                  _const_spec((1, D_RNN)), _const_spec((D_RNN, d)), _const_spec((1, d)), _const_spec((1, d))],
        out_specs=[pl.BlockSpec((1, t_rows, d), xmap), pl.BlockSpec((1, t_rows, d), xmap),
                   pl.BlockSpec((1, SUBLANES, D_RNN), smap), pl.BlockSpec((1, SUBLANES, D_RNN), smap)],
        out_shape=[jax.ShapeDtypeStruct((bsz, seq, d), F32), jax.ShapeDtypeStruct((bsz, seq, d), BF16),
                   jax.ShapeDtypeStruct((bsz, SUBLANES, D_RNN), F32),
                   jax.ShapeDtypeStruct((bsz, SUBLANES, D_RNN), F32)],
        scratch_shapes=[pltpu.VMEM((SUBLANES, D_RNN), F32), pltpu.VMEM((1, D_RNN), F32),
                        pltpu.VMEM((t_rows, D_RNN), F32), pltpu.VMEM((t_rows, D_RNN), F32)],
        compiler_params=_params("parallel", "arbitrary"),
        name="rglru_prompt",
    )(x_bf, x, w_in, conv_w, conv_b.reshape(1, D_RNN), wband, b_gate, lam.reshape(1, D_RNN), w_out,
      g.reshape(1, d), b.reshape(1, d))


def _rglru_sample_kernel(xbf_ref, x_ref, st_ref, h0_ref, win_ref, cw_ref, cb_ref, wband_ref, bg_ref, lam_ref,
                         wout_ref, g_ref, b_ref, y_ref, newst_ref, hlast_ref, h_sc, *, n_new):
    bd = h0_ref.shape[0]
    xb16 = xbf_ref[...]
    gate_br = _dot(xb16, win_ref[:, 0:D_RNN])
    xb = _dot(xb16, win_ref[:, D_RNN:2 * D_RNN])
    xpad = jnp.concatenate([st_ref[jj] for jj in range(CONV_W - 1)] + [xb], axis=0)
    xc = cb_ref[...]
    for jj in range(CONV_W):
        xc = xc + cw_ref[jj:jj + 1, :] * xpad[jj * bd:(jj + n_new) * bd]
    for jj in range(CONV_W - 1):
        newst_ref[jj] = xpad[(n_new + jj) * bd:(n_new + jj + 1) * bd]
    xc_bf = xc.astype(BF16)
    for j in range(N_GATE_TILES):
        lo = j * GATE_TILE
        w = min(GATE_TILE, D_RNN - lo)
        a, u = _rglru_gate_tile(j, xc, xc_bf, wband_ref, bg_ref, lam_ref)
        h = h0_ref[:, lo:lo + w]
        for t in range(n_new):
            h = a[t * bd:(t + 1) * bd] * h + u[t * bd:(t + 1) * bd]
            h_sc[t * bd:(t + 1) * bd, lo:lo + w] = h
        hlast_ref[:, lo:lo + w] = h
    mixed = (h_sc[...] * jax.nn.gelu(gate_br)).astype(BF16)
    y_ref[...] = _layer_norm(ALPHA * x_ref[...] + _dot(mixed, wout_ref[...]), g_ref[...], b_ref[...])


def rglru_sample(x_bf_tm, x_tm, state_tm, h0, w_in, conv_w, conv_b, wband, b_gate, lam, w_out, g, b):
    m, d = x_tm.shape
    bd = h0.shape[0]
    full = lambda shape: pl.BlockSpec(shape, lambda: (0,) * len(shape))
    args = (x_bf_tm, x_tm, state_tm, h0, w_in, conv_w, conv_b.reshape(1, D_RNN), wband, b_gate,
            lam.reshape(1, D_RNN), w_out, g.reshape(1, d), b.reshape(1, d))
    return pl.pallas_call(
        functools.partial(_rglru_sample_kernel, n_new=m // bd),
        in_specs=[full(a.shape) for a in args],
        out_specs=[full((m, d)), full((CONV_W - 1, bd, D_RNN)), full((bd, D_RNN))],
        out_shape=[jax.ShapeDtypeStruct((m, d), F32), jax.ShapeDtypeStruct((CONV_W - 1, bd, D_RNN), F32),
                   jax.ShapeDtypeStruct((bd, D_RNN), F32)],
        scratch_shapes=[pltpu.VMEM((m, D_RNN), F32)],
        compiler_params=pltpu.CompilerParams(vmem_limit_bytes=VMEM_LIMIT),
        name="rglru_sample",
    )(*args)


def _rope_table_kernel(cos_ref, sin_ref, *, pos0, period):
    n, half = cos_ref.shape
    r = lax.broadcasted_iota(jnp.int32, (n, half), 0)
    i = lax.broadcasted_iota(jnp.int32, (n, half), 1)
    pos = (pos0 + r % period).astype(F32)
    inv = jnp.exp(i.astype(F32) * (-math.log(ROPE_BASE) / half))
    ang = pos * inv
    cos_ref[...] = jnp.cos(ang)
    sin_ref[...] = jnp.sin(ang)


def rope_tables(n, pos0, period):
    half = DK_C // 2
    spec = pl.BlockSpec((n, half), lambda: (0, 0))
    return pl.pallas_call(
        functools.partial(_rope_table_kernel, pos0=pos0, period=period),
        out_specs=[spec, spec],
        out_shape=[jax.ShapeDtypeStruct((n, half), F32)] * 2,
        name="rope_tables",
    )()


def _ret_proj_kernel(x_ref, w_ref, cos_ref, sin_ref, q_ref, k_ref, v_ref, g_ref):
    x = x_ref[0]
    cos = cos_ref[...]
    sin = sin_ref[...]
    half = DK_C // 2
    qk_w = H_C * DK_C
    v_w = H_C * DV_C

    def roped(y, scale):
        outs = []
        for h in range(H_C):
            x1 = y[:, h * DK_C:h * DK_C + half]
            x2 = y[:, h * DK_C + half:(h + 1) * DK_C]
            outs += [x1 * cos - x2 * sin, x1 * sin + x2 * cos]
        return (jnp.concatenate(outs, axis=1) * scale).astype(BF16)

    q_ref[0] = roped(_dot(x, w_ref[:, 0:qk_w]), 1.0)
    k_ref[0] = roped(_dot(x, w_ref[:, qk_w:2 * qk_w]), DK_C ** -0.5)
    v_ref[0] = _dot(x, w_ref[:, 2 * qk_w:2 * qk_w + v_w]).astype(BF16)
    g_ref[0] = _dot(x, w_ref[:, 2 * qk_w + v_w:2 * qk_w + 2 * v_w])


def ret_proj(x_bf, w_in, cos, sin, tm):
    bsz, seq, d = x_bf.shape
    qk_w, v_w = H_C * DK_C, H_C * DV_C
    xmap = lambda b, i: (b, i, 0)
    tmap = lambda b, i: (i, 0)
    return pl.pallas_call(
        _ret_proj_kernel,
        grid=(bsz, seq // tm),
        in_specs=[pl.BlockSpec((1, tm, d), xmap), _const_spec((d, 2 * qk_w + 2 * v_w)),
                  pl.BlockSpec((tm, DK_C // 2), tmap), pl.BlockSpec((tm, DK_C // 2), tmap)],
        out_specs=[pl.BlockSpec((1, tm, qk_w), xmap), pl.BlockSpec((1, tm, qk_w), xmap),
                   pl.BlockSpec((1, tm, v_w), xmap), pl.BlockSpec((1, tm, v_w), xmap)],
        out_shape=[jax.ShapeDtypeStruct((bsz, seq, qk_w), BF16), jax.ShapeDtypeStruct((bsz, seq, qk_w), BF16),
                   jax.ShapeDtypeStruct((bsz, seq, v_w), BF16), jax.ShapeDtypeStruct((bsz, seq, v_w), F32)],
        compiler_params=_params("parallel", "parallel"),
        name="ret_proj",
    )(x_bf, w_in, cos, sin)


def _retention_kernel(lg_ref, q_ref, k_ref, v_ref, g_ref, gng_ref, gnb_ref, s0_ref, o_ref, sout_ref, s_sc,
                      *, t_true, n_heads):
    c = pl.program_id(2)

    @pl.when(c == 0)
    def _load_state():
        s_sc[...] = s0_ref[...]

    t = q_ref.shape[1]
    ti = lax.broadcasted_iota(jnp.int32, (t, t), 0)
    si = lax.broadcasted_iota(jnp.int32, (t, t), 1)
    diff = (ti - si).astype(F32)
    tcol = lax.broadcasted_iota(jnp.int32, (t, 1), 0).astype(F32)
    for hh in range(n_heads):
        lg = lg_ref[hh, :, 0:1]
        q = q_ref[0, :, hh * DK_C:(hh + 1) * DK_C]
        k = k_ref[0, :, hh * DK_C:(hh + 1) * DK_C]
        v = v_ref[0, :, hh * DV_C:(hh + 1) * DV_C]
        decay = jnp.where(diff >= 0, jnp.exp(jnp.maximum(diff, 0.0) * lg), 0.0)
        scores = lax.dot_general(q, k, _NT, preferred_element_type=F32) * decay
        o = _dot(scores.astype(BF16), v)
        state = s_sc[hh]
        o = o + _dot(q, state.astype(BF16)) * jnp.exp((tcol + 1.0) * lg)
        zeta = jnp.exp((t_true - 1.0 - tcol) * lg)
        kz = (k.astype(F32) * zeta).astype(BF16)
        new_state = jnp.exp(t_true * lg) * state + lax.dot_general(kz, v, _TN, preferred_element_type=F32)
        s_sc[hh] = new_state

        mu = jnp.mean(o, axis=-1, keepdims=True)
        oc = o - mu
        var = jnp.mean(oc * oc, axis=-1, keepdims=True)
        cols = slice(hh * DV_C, (hh + 1) * DV_C)
        on = oc * lax.rsqrt(var + GN_EPS) * gng_ref[:, cols] + gnb_ref[:, cols]
        o_ref[0, :, cols] = (jax.nn.silu(g_ref[0, :, cols]) * on).astype(BF16)

    @pl.when(c == pl.num_programs(2) - 1)
    def _store_state():
        sout_ref[...] = s_sc[...]


def retention(log_g, q, k, v, g, gn_g, gn_b, s0, chunk, t_true, n_heads):
    bsz, seq, _ = q.shape
    v_w = H_C * DV_C
    qmap = lambda b, h, c: (b, c, h)
    smap = lambda b, h, c: (b, h, 0, 0)
    hmap = lambda b, h, c: (0, h)
    return pl.pallas_call(
        functools.partial(_retention_kernel, t_true=float(t_true), n_heads=n_heads),
        grid=(bsz, H_C // n_heads, seq // chunk),
        in_specs=[pl.BlockSpec((n_heads, 1, LANES), lambda b, h, c: (h, 0, 0)),
                  pl.BlockSpec((1, chunk, n_heads * DK_C), qmap), pl.BlockSpec((1, chunk, n_heads * DK_C), qmap),
                  pl.BlockSpec((1, chunk, n_heads * DV_C), qmap), pl.BlockSpec((1, chunk, n_heads * DV_C), qmap),
                  pl.BlockSpec((1, n_heads * DV_C), hmap), pl.BlockSpec((1, n_heads * DV_C), hmap),
                  pl.BlockSpec((None, n_heads, DK_C, DV_C), smap)],
        out_specs=[pl.BlockSpec((1, chunk, n_heads * DV_C), qmap),
                   pl.BlockSpec((None, n_heads, DK_C, DV_C), smap)],
        out_shape=[jax.ShapeDtypeStruct((bsz, seq, v_w), BF16),
                   jax.ShapeDtypeStruct((bsz, H_C, DK_C, DV_C), F32)],
        scratch_shapes=[pltpu.VMEM((n_heads, DK_C, DV_C), F32)],
        compiler_params=_params("parallel", "parallel", "arbitrary"),
        name="retention",
    )(log_g, q, k, v, g, gn_g.reshape(1, v_w), gn_b.reshape(1, v_w), s0)


TM_DENSE = 512
TQ_ATTN = 512
T_RGLRU = 512
PAGE_GROUP = 16
RET_CHUNK = 256
ATTN_HEAD_PAIRS = 4
RET_HEADS = 4


def kernel(x_prompt, x_sample, cache_k_a, cache_v_a, cache_logf_a, page_table, state_conv_b, state_h_b, state_ret_c, w_in_a, b_f_a, w_out_a, w_in_b, conv_w_b, conv_b_b, w_gate_b, b_gate_b, lam_b, w_out_b, w_in_c, gn_g_c, gn_b_c, w_out_c, ln1_g, ln1_b, w_up_ffn, w_down_ffn, ln2_g, ln2_b):
    bsz, seq, d = x_prompt.shape
    bd, n_new, _ = x_sample.shape
    mp, ms = bsz * seq, bd * n_new
    past_len = page_table.shape[1] * cache_k_a.shape[2]

    xp = x_prompt.reshape(mp, d)
    xs = x_sample.reshape(ms, d)
    xp_bf = None
    xs_bf = xs.astype(BF16)

    n_la, pool, page = cache_k_a.shape[:3]
    kT_pool = jnp.transpose(cache_k_a, (0, 1, 3, 4, 2)).reshape(n_la, pool, d, page)
    vT_pool = jnp.transpose(cache_v_a, (0, 1, 3, 4, 2)).reshape(n_la, pool, d, page)
    lfT_pool = jnp.transpose(cache_logf_a, (0, 1, 3, 2))

    log_g = jnp.log1p(-jnp.exp2(-5.0 - jnp.arange(H_C, dtype=F32)))
    log_g = jnp.broadcast_to(log_g[:, None, None], (H_C, 1, LANES))

    w_up_all = w_up_ffn.astype(BF16)
    w_dn_all = w_down_ffn.astype(BF16)

    outs_a = {name: [] for name in ("lfp", "ks", "vs", "lfs")}
    out_b, out_c = {}, {}
    kv_all = None

    for i in range(DEPTH):
        jl, kind = divmod(i, 3)
        if kind == 0:
            w = w_in_a[jl]
            wq = w[:, 0:d].astype(BF16)
            wkT = w[:, d:2 * d].T.astype(BF16)
            wvT = w[:, 2 * d:3 * d].T.astype(BF16)
            wfT = w[:, 3 * d:].T.astype(BF16)
            w_out = w_out_a[jl].astype(BF16)
            q_bf, kT_all, vT_all, kT_bf, vT_bf, lfT = fox_proj(
                (xp if xp_bf is None else xp_bf).reshape(bsz, seq, d), wq, wkT, wvT, wfT, b_f_a[jl].reshape(H_A, 1), TM_DENSE,
                jl, n_la, kv_all)
            kv_all = (kT_all, vT_all)
            ct = fox_cumsum(lfT)
            o_bf = fox_attn(q_bf, kT_bf, vT_bf, jnp.transpose(ct, (0, 2, 1)), ct, TQ_ATTN, ATTN_HEAD_PAIRS)
            mix_p = (o_bf.reshape(mp, d), w_out)
            outs_a["lfp"].append(lfT)
            wf_pad = jnp.pad(w[:, 3 * d:], ((0, 0), (0, LANES - H_A))).astype(BF16)
            bf_pad = jnp.pad(b_f_a[jl], (0, LANES - H_A)).reshape(1, LANES)
            qs, ks, vs, lfs = fox_proj_sample(xs_bf, w[:, 0:3 * d].astype(BF16), wf_pad, bf_pad)
            lfs = lfs[:, :H_A].reshape(bd, n_new, H_A)
            lfs_T = jnp.pad(jnp.transpose(lfs, (0, 2, 1)), ((0, 0), (0, 0), (0, LANES - n_new)))
            o_s = fox_sample_attn(page_table, kT_pool, vT_pool, lfT_pool, jl, qs.reshape(bd, n_new, d),
                                  ks.reshape(bd, n_new, d), vs.reshape(bd, n_new, d), lfs_T, PAGE_GROUP)
            mix_s = (o_s.reshape(ms, d).astype(BF16), w_out)
            outs_a["ks"].append(ks.reshape(bd, n_new, H_A, DH_A))
            outs_a["vs"].append(vs.reshape(bd, n_new, H_A, DH_A))
            outs_a["lfs"].append(lfs)
        elif kind == 1:
            w_in = w_in_b[jl].astype(BF16)
            w_out = w_out_b[jl].astype(BF16)
            wband = _band_gate_weights(w_gate_b[jl])
            xp3, xp_bf3, tail, hlast = rglru_prompt(
                xp_bf.reshape(bsz, seq, d), xp.reshape(bsz, seq, d), w_in, conv_w_b[jl], conv_b_b[jl], wband,
                b_gate_b[jl], lam_b[jl], w_out, ln1_g[i], ln1_b[i], T_RGLRU)
            xp, xp_bf = xp3.reshape(mp, d), xp_bf3.reshape(mp, d)
            out_b["cp"] = tail[:, SUBLANES - (CONV_W - 1):, :]
            out_b["hp"] = hlast[:, 0, :]
            to_tm = lambda a: jnp.transpose(a.reshape(bd, n_new, d), (1, 0, 2)).reshape(ms, d)
            ys_tm, newst, hs = rglru_sample(
                to_tm(xs_bf), to_tm(xs), jnp.transpose(state_conv_b[jl], (1, 0, 2)), state_h_b[jl], w_in,
                conv_w_b[jl], conv_b_b[jl], wband, b_gate_b[jl], lam_b[jl], w_out, ln1_g[i], ln1_b[i])
            xs = jnp.transpose(ys_tm.reshape(n_new, bd, d), (1, 0, 2)).reshape(ms, d)
            xs_bf = xs.astype(BF16)
            out_b["cs"] = jnp.transpose(newst, (1, 0, 2))
            out_b["hs"] = hs
            mix_p = mix_s = None
        else:
            w_in = w_in_c[jl].astype(BF16)
            w_out = w_out_c[jl].astype(BF16)
            v_w = H_C * DV_C
            cos_p, sin_p = rope_tables(seq, 0, seq)
            q, k, v, g = ret_proj(xp_bf.reshape(bsz, seq, d), w_in, cos_p, sin_p, TM_DENSE)
            chunk = next((c for c in (RET_CHUNK, CHUNK_C) if seq % c == 0), seq)
            og, s_p = retention(log_g, q, k, v, g, gn_g_c[jl], gn_b_c[jl],
                                jnp.zeros((bsz, H_C, DK_C, DV_C), F32), chunk, chunk, RET_HEADS)
            mix_p = (og.reshape(mp, v_w), w_out)
            out_c["sp"] = s_p
            cos_s, sin_s = rope_tables(ms, past_len, n_new)
            q, k, v, g = ret_proj(xs_bf.reshape(1, ms, d), w_in, cos_s, sin_s, ms)
            pad_rows = lambda a: jnp.pad(a.reshape(bd, n_new, a.shape[-1]), ((0, 0), (0, SUBLANES - n_new), (0, 0)))
            og, s_s = retention(log_g, pad_rows(q), pad_rows(k), pad_rows(v), pad_rows(g), gn_g_c[jl], gn_b_c[jl],
                                state_ret_c[jl], SUBLANES, n_new, RET_HEADS)
            mix_s = (og[:, :n_new].reshape(ms, v_w), w_out)
            out_c["ss"] = s_s
        if mix_p is not None:
            xp, xp_bf = proj_ln(mix_p[0], mix_p[1], xp, ln1_g[i], ln1_b[i], TM_DENSE)
            xs, xs_bf = proj_ln(mix_s[0], mix_s[1], xs, ln1_g[i], ln1_b[i], TM_DENSE)
        xp, xp_bf = ffn_ln(xp_bf, xp, w_up_all, w_dn_all, i, ln2_g[i], ln2_b[i], TM_DENSE)
        xs, xs_bf = ffn_ln(xs_bf, xs, w_up_all, w_dn_all, i, ln2_g[i], ln2_b[i], TM_DENSE)

    kv_out = lambda a: jnp.transpose(a.reshape(n_la, bsz, H_A, DH_A, seq), (0, 1, 4, 2, 3))
    return (xp.reshape(bsz, seq, d), xs.reshape(bd, n_new, d),
            kv_out(kv_all[0]), kv_out(kv_all[1]), jnp.transpose(jnp.stack(outs_a["lfp"]), (0, 1, 3, 2)),
            jnp.stack(outs_a["ks"]), jnp.stack(outs_a["vs"]), jnp.stack(outs_a["lfs"]),
            out_b["cp"][None], out_b["hp"][None], out_b["cs"][None], out_b["hs"][None],
            out_c["sp"][None], out_c["ss"][None])
```
